```python
import math
import jax
import jax.numpy as jnp
from jax import lax
import numpy as np

D_MODEL = 1024
BATCH = 8
SEQ = 4096
DEPTH = 2

EPS = 1e-6
PLE_DIM = 256
D_FF = 2816

CONV_CH = 1024
CONV_WIDTH = 31

SSM_HEADS = 16
SSM_HEAD_DIM = 64
SSM_INNER = SSM_HEADS * SSM_HEAD_DIM
SSM_GROUPS = 2
SSM_STATE = 128
SSM_CONV = 4
SSM_CHUNK = 128
SSM_XBC = SSM_INNER + 2 * SSM_GROUPS * SSM_STATE

HYB_IN = 2 * CONV_CH + SSM_INNER + SSM_XBC + SSM_HEADS
HYB_MIX = CONV_CH + SSM_INNER

ATT_HEADS = 16
ATT_KV_HEADS = 4
ATT_HEAD_DIM = 64
ATT_QKV = (ATT_HEADS + 2 * ATT_KV_HEADS) * ATT_HEAD_DIM
WINDOW = 128
ROPE_THETA = 10000.0

N_EVEN = (DEPTH + 1) // 2
N_ODD = DEPTH // 2

kernel_name = 'macaron_conv_ssd_swa_hybrid'


def rms_norm(x, g):
    xf = x.astype(jnp.float32)
    y = xf * lax.rsqrt(jnp.mean(xf * xf, axis=-1, keepdims=True) + EPS)
    return (y * g.astype(jnp.float32)).astype(x.dtype)


def layer_norm(x, g, b):
    xf = x.astype(jnp.float32)
    mu = jnp.mean(xf, axis=-1, keepdims=True)
    xc = xf - mu
    var = jnp.mean(xc * xc, axis=-1, keepdims=True)
    y = xc * lax.rsqrt(var + EPS) * g.astype(jnp.float32) + b.astype(jnp.float32)
    return y.astype(x.dtype)


def grouped_rms_norm(y, g):
    bsz, seqlen, ch = y.shape
    yg = y.reshape(bsz, seqlen, SSM_GROUPS, ch // SSM_GROUPS)
    yg = yg * lax.rsqrt(jnp.mean(yg * yg, axis=-1, keepdims=True) + EPS)
    return yg.reshape(bsz, seqlen, ch) * g.astype(jnp.float32)


def swiglu(x, w_in, w_out):
    gate, up = jnp.split(x @ w_in, 2, axis=-1)
    return (jax.nn.silu(gate) * up) @ w_out


def causal_depthwise_conv(x, w, b):
    width, ch = w.shape
    y = lax.conv_general_dilated(x, w[:, None, :], window_strides=(1,), padding=[(width - 1, 0)],
                                 dimension_numbers=('NWC', 'WIO', 'NWC'), feature_group_count=ch)
    return y + b


def segsum_exp(a):
    t = a.shape[-1]
    cs = jnp.cumsum(a, axis=-1)
    diff = cs[..., :, None] - cs[..., None, :]
    mask = jnp.tril(jnp.ones((t, t), dtype=bool))
    return jnp.exp(jnp.where(mask, diff, -jnp.inf))


def ssd_chunked(x, dt, a, b, c):
    bsz, seqlen = x.shape[:2]
    nc = seqlen // SSM_CHUNK
    r = SSM_HEADS // SSM_GROUPS
    ln = SSM_CHUNK
    xdt = (x * dt[..., None]).reshape(bsz, nc, ln, SSM_GROUPS, r, SSM_HEAD_DIM)
    adt = (dt * a).reshape(bsz, nc, ln, SSM_GROUPS, r).transpose(0, 3, 4, 1, 2)
    a_cs = jnp.cumsum(adt, axis=-1)
    bc = b.reshape(bsz, nc, ln, SSM_GROUPS, SSM_STATE)
    cc = c.reshape(bsz, nc, ln, SSM_GROUPS, SSM_STATE)
    cb = jnp.einsum('bclgn,bcsgn->bcgls', cc, bc)
    y_diag = jnp.einsum('bcgls,bgrcls,bcsgrp->bclgrp', cb, segsum_exp(adt), xdt)
    decay_to_end = jnp.exp(a_cs[..., -1:] - a_cs)
    states = jnp.einsum('bclgn,bgrcl,bclgrp->bcgrpn', bc, decay_to_end, xdt)
    chunk_decay = jnp.exp(a_cs[..., -1])

    def step(h, inp):
        s_c, d_c = inp
        return h * d_c[..., None, None] + s_c, h

    h0 = jnp.zeros((bsz, SSM_GROUPS, r, SSM_HEAD_DIM, SSM_STATE), jnp.float32)
    _, prev = lax.scan(step, h0, (jnp.moveaxis(states, 1, 0), jnp.moveaxis(chunk_decay, 3, 0)))
    y_off = jnp.einsum('bclgn,cbgrpn,bgrcl->bclgrp', cc, prev, jnp.exp(a_cs))
    return (y_diag + y_off).reshape(bsz, seqlen, SSM_HEADS, SSM_HEAD_DIM)


def conv_ssd_mixer(hn, w_in, cv_w, cv_b, cv_g, cv_beta, sc_w, sc_b, dt_bias, a_log, d_skip, ssm_norm, w_out):
    bsz, seqlen, _ = hn.shape
    f32 = jnp.float32
    cut = np.cumsum([CONV_CH, CONV_CH, SSM_INNER, SSM_XBC]).tolist()
    cv_val, cv_gate, z, xbc, dt_raw = jnp.split(hn @ w_in, cut, axis=-1)
    u = cv_val * jax.nn.sigmoid(cv_gate)
    u = causal_depthwise_conv(u, cv_w, cv_b)
    u = jax.nn.silu(layer_norm(u, cv_g, cv_beta))
    xbc = jax.nn.silu(causal_depthwise_conv(xbc, sc_w, sc_b))
    xs, bs, cs = jnp.split(xbc, [SSM_INNER, SSM_INNER + SSM_GROUPS * SSM_STATE], axis=-1)
    xs = xs.reshape(bsz, seqlen, SSM_HEADS, SSM_HEAD_DIM).astype(f32)
    dt = jax.nn.softplus(dt_raw.astype(f32) + dt_bias.astype(f32))
    a = -jnp.exp(a_log.astype(f32))
    y = ssd_chunked(xs, dt, a,
                    bs.reshape(bsz, seqlen, SSM_GROUPS, SSM_STATE).astype(f32),
                    cs.reshape(bsz, seqlen, SSM_GROUPS, SSM_STATE).astype(f32))
    y = y + d_skip.astype(f32)[:, None] * xs
    y = y.reshape(bsz, seqlen, SSM_INNER) * jax.nn.silu(z.astype(f32))
    y = grouped_rms_norm(y, ssm_norm).astype(hn.dtype)
    return jnp.concatenate([u, y], axis=-1) @ w_out


def rope_tables(seqlen):
    inv = ROPE_THETA ** (-jnp.arange(0, ATT_HEAD_DIM, 2, dtype=jnp.float32) / ATT_HEAD_DIM)
    ang = jnp.arange(seqlen, dtype=jnp.float32)[:, None] * inv[None, :]
    return jnp.cos(ang), jnp.sin(ang)


def apply_rope(t, cos, sin):
    half = t.shape[-1] // 2
    t1, t2 = t[..., :half], t[..., half:]
    c = cos[None, :, None, :].astype(t.dtype)
    s = sin[None, :, None, :].astype(t.dtype)
    return jnp.concatenate([t1 * c - t2 * s, t2 * c + t1 * s], axis=-1)


def swa_sink_attention(hn, w_qkv, b_qkv, sinks, w_o, b_o, cos, sin):
    bsz, seqlen, _ = hn.shape
    nb = seqlen // WINDOW
    grp = ATT_HEADS // ATT_KV_HEADS
    q, k, v = jnp.split(hn @ w_qkv + b_qkv,
                        [ATT_HEADS * ATT_HEAD_DIM, (ATT_HEADS + ATT_KV_HEADS) * ATT_HEAD_DIM], axis=-1)
    q = apply_rope(q.reshape(bsz, seqlen, ATT_HEADS, ATT_HEAD_DIM), cos, sin)
    k = apply_rope(k.reshape(bsz, seqlen, ATT_KV_HEADS, ATT_HEAD_DIM), cos, sin)
    v = v.reshape(bsz, seqlen, ATT_KV_HEADS, ATT_HEAD_DIM)
    qb = q.reshape(bsz, nb, WINDOW, ATT_KV_HEADS, grp, ATT_HEAD_DIM)

    def band(t):
        tp = jnp.pad(t, ((0, 0), (WINDOW, 0), (0, 0), (0, 0)))
        tp = tp.reshape(bsz, nb + 1, WINDOW, ATT_KV_HEADS, ATT_HEAD_DIM)
        return jnp.concatenate([tp[:, :-1], tp[:, 1:]], axis=2)

    kb, vb = band(k), band(v)
    logits = jnp.einsum('bnqkgd,bnskd->bnkgqs', qb, kb).astype(jnp.float32) * (ATT_HEAD_DIM ** -0.5)
    qpos = jnp.arange(nb)[:, None, None] * WINDOW + jnp.arange(WINDOW)[None, :, None]
    kpos = jnp.arange(nb)[:, None, None] * WINDOW - WINDOW + jnp.arange(2 * WINDOW)[None, None, :]
    rel = qpos - kpos
    valid = (kpos >= 0) & (rel >= 0) & (rel < WINDOW)
    logits = jnp.where(valid[None, :, None, None], logits, -jnp.inf)
    sink = sinks.astype(jnp.float32).reshape(ATT_KV_HEADS, grp)[None, None, :, :, None, None]
    m = jnp.maximum(jnp.max(logits, axis=-1, keepdims=True), sink)
    e = jnp.exp(logits - m)
    probs = e / (jnp.sum(e, axis=-1, keepdims=True) + jnp.exp(sink - m))
    o = jnp.einsum('bnkgqs,bnskd->bnqkgd', probs.astype(vb.dtype), vb)
    return o.reshape(bsz, seqlen, ATT_HEADS * ATT_HEAD_DIM) @ w_o + b_o


def setup_inputs(seed: int = 0) -> dict:
    key = jax.random.key(seed)
    ks = iter(jax.random.split(key, 40))
    D = D_MODEL

    def nrm(shape, scale):
        return scale * jax.random.normal(next(ks), shape, jnp.float32)

    def gain(shape):
        return 1.0 + nrm(shape, 0.02)

    x = jax.random.normal(next(ks), (BATCH, SEQ, D), jnp.float32)
    p = jax.random.normal(next(ks), (DEPTH, BATCH, SEQ, PLE_DIM), jnp.float32)
    dt0 = jnp.exp(jax.random.uniform(next(ks), (N_EVEN, SSM_HEADS), jnp.float32,
                                     minval=math.log(1e-3), maxval=math.log(1e-1)))
    a_init = jax.random.uniform(next(ks), (N_EVEN, SSM_HEADS), jnp.float32, minval=1.0, maxval=16.0)
    return {
        'x': x,
        'p': p,
        'norm_ffn1': gain((DEPTH, D)),
        'ffn1_w_in': nrm((DEPTH, D, 2 * D_FF), D ** -0.5),
        'ffn1_w_out': nrm((DEPTH, D_FF, D), D_FF ** -0.5),
        'norm_mix': gain((DEPTH, D)),
        'norm_ffn2': gain((DEPTH, D)),
        'ffn2_w_in': nrm((DEPTH, D, 2 * D_FF), D ** -0.5),
        'ffn2_w_out': nrm((DEPTH, D_FF, D), D_FF ** -0.5),
        'ple_norm': gain((DEPTH, D)),
        'ple_gate_w': nrm((DEPTH, D, D), D ** -0.5),
        'ple_proj_w': nrm((DEPTH, PLE_DIM, D), PLE_DIM ** -0.5),
        'hyb_w_in': nrm((N_EVEN, D, HYB_IN), D ** -0.5),
        'conv_dw_w': nrm((N_EVEN, CONV_WIDTH, CONV_CH), CONV_WIDTH ** -0.5),
        'conv_dw_b': nrm((N_EVEN, CONV_CH), 0.02),
        'conv_ln_g': gain((N_EVEN, CONV_CH)),
        'conv_ln_b': nrm((N_EVEN, CONV_CH), 0.02),
        'ssm_conv_w': nrm((N_EVEN, SSM_CONV, SSM_XBC), SSM_CONV ** -0.5),
        'ssm_conv_b': nrm((N_EVEN, SSM_XBC), 0.02),
        'ssm_dt_bias': dt0 + jnp.log(-jnp.expm1(-dt0)),
        'ssm_a_log': jnp.log(a_init),
        'ssm_d': gain((N_EVEN, SSM_HEADS)),
        'ssm_norm': gain((N_EVEN, SSM_INNER)),
        'hyb_w_out': nrm((N_EVEN, HYB_MIX, D), HYB_MIX ** -0.5),
        'att_w_qkv': nrm((N_ODD, D, ATT_QKV), D ** -0.5),
        'att_b_qkv': nrm((N_ODD, ATT_QKV), 0.02),
        'att_sinks': nrm((N_ODD, ATT_HEADS), 0.5),
        'att_w_o': nrm((N_ODD, ATT_HEADS * ATT_HEAD_DIM, D), (ATT_HEADS * ATT_HEAD_DIM) ** -0.5),
        'att_b_o': nrm((N_ODD, D), 0.02),
        'final_norm': gain((D,)),
    }


def reference(x, p, norm_ffn1, ffn1_w_in, ffn1_w_out, norm_mix, norm_ffn2, ffn2_w_in, ffn2_w_out,
              ple_norm, ple_gate_w, ple_proj_w, hyb_w_in, conv_dw_w, conv_dw_b, conv_ln_g, conv_ln_b,
              ssm_conv_w, ssm_conv_b, ssm_dt_bias, ssm_a_log, ssm_d, ssm_norm, hyb_w_out,
              att_w_qkv, att_b_qkv, att_sinks, att_w_o, att_b_o, final_norm):
    cos, sin = rope_tables(x.shape[1])
    h = x
    for i in range(DEPTH):
        j = i // 2
        h = h + 0.5 * swiglu(rms_norm(h, norm_ffn1[i]), ffn1_w_in[i], ffn1_w_out[i])
        hn = rms_norm(h, norm_mix[i])
        if i % 2 == 0:
            h = h + conv_ssd_mixer(hn, hyb_w_in[j], conv_dw_w[j], conv_dw_b[j], conv_ln_g[j], conv_ln_b[j],
                                   ssm_conv_w[j], ssm_conv_b[j], ssm_dt_bias[j], ssm_a_log[j], ssm_d[j],
                                   ssm_norm[j], hyb_w_out[j])
        else:
            h = h + swa_sink_attention(hn, att_w_qkv[j], att_b_qkv[j], att_sinks[j], att_w_o[j], att_b_o[j],
                                       cos, sin)
        h = h + 0.5 * swiglu(rms_norm(h, norm_ffn2[i]), ffn2_w_in[i], ffn2_w_out[i])
        gate = jax.nn.sigmoid(rms_norm(h, ple_norm[i]) @ ple_gate_w[i])
        h = h + gate * (p[i] @ ple_proj_w[i])
    return rms_norm(h, final_norm)
```

```python
import functools

import jax
import jax.numpy as jnp
from jax import lax
from jax.experimental import pallas as pl
from jax.experimental.pallas import tpu as pltpu

F32 = jnp.float32
BF16 = jnp.bfloat16

EPS = 1e-6
LANES = 128
SUBLANES = 8
VMEM_LIMIT_BYTES = 56 * 1024 * 1024

CONV_WIDTH = 31
CONV_HALO = 32
SSM_HEADS = 16
SSM_HEAD_DIM = 64
SSM_GROUPS = 2
SSM_STATE = 128
SSM_CONV = 4
SSM_HALO = 8
SSM_CHUNK = 128
ATT_HEADS = 16
ATT_KV_HEADS = 4
ATT_HEAD_DIM = 64
WINDOW = 128
ROPE_THETA = 10000.0

FFN_TM = 512
FFN_FC = 256
HYB_TS = 256
ATT_TS = 512


def _rms(x, g):
    return x * lax.rsqrt(jnp.mean(x * x, axis=-1, keepdims=True) + EPS) * g


def _silu(x):
    return x * jax.nn.sigmoid(x)


def _dot(a, b):
    return jnp.dot(a, b, preferred_element_type=F32)


def _dot_nt(a, b):
    return lax.dot_general(a, b, (((1,), (1,)), ((), ())), preferred_element_type=F32)


def _const_spec(shape):
    nd = len(shape)
    return pl.BlockSpec(shape, lambda *_: (0,) * nd, pipeline_mode=pl.Buffered(1))


def _split_bf16(x, parts):
    out = []
    r = x
    for _ in range(parts):
        p = r.astype(BF16)
        out.append(p)
        r = r - p.astype(F32)
    return out


def _ffn_kernel(*refs, d_ff, fc, with_ple, with_final):
    h_ref, g_ref, win_ref, wout_ref = refs[:4]
    rest = list(refs[4:])
    if with_ple:
        p_ref, pn_ref, pgw_ref, ppw_ref = rest[:4]
        rest = rest[4:]
    if with_final:
        fn_ref = rest[0]
        rest = rest[1:]
    (o_ref,) = rest

    h = h_ref[...]
    xn = _rms(h, g_ref[...]).astype(BF16)
    acc = jnp.zeros(h.shape, F32)
    for c in range(d_ff // fc):
        gate = _dot(xn, win_ref[:, c * fc:(c + 1) * fc])
        up = _dot(xn, win_ref[:, d_ff + c * fc:d_ff + (c + 1) * fc])
        act = (_silu(gate) * up).astype(BF16)
        acc = acc + _dot(act, wout_ref[c * fc:(c + 1) * fc, :])
    h = h + 0.5 * acc
    if with_ple:
        gate = jax.nn.sigmoid(_dot(_rms(h, pn_ref[...]).astype(BF16), pgw_ref[...]))
        h = h + gate * _dot(p_ref[...].astype(BF16), ppw_ref[...])
    if with_final:
        h = _rms(h, fn_ref[...])
    o_ref[...] = h


def _ffn(h, g, w_in, w_out, ple=None, final_g=None, *, name):
    m, d = h.shape
    d_ff = w_out.shape[0]
    tm = FFN_TM
    row = lambda i: (i, 0)
    in_specs = [pl.BlockSpec((tm, d), row), _const_spec((1, d)),
                _const_spec(w_in.shape), _const_spec(w_out.shape)]
    args = [h, g.reshape(1, d), w_in, w_out]
    if ple is not None:
        p, pn, pgw, ppw = ple
        in_specs += [pl.BlockSpec((tm, p.shape[1]), row), _const_spec((1, d)),
                     _const_spec(pgw.shape), _const_spec(ppw.shape)]
        args += [p, pn.reshape(1, d), pgw, ppw]
    if final_g is not None:
        in_specs.append(_const_spec((1, d)))
        args.append(final_g.reshape(1, d))
    return pl.pallas_call(
        functools.partial(_ffn_kernel, d_ff=d_ff, fc=FFN_FC, with_ple=ple is not None,
                          with_final=final_g is not None),
        grid=(m // tm,),
        in_specs=in_specs,
        out_specs=pl.BlockSpec((tm, d), row),
        out_shape=jax.ShapeDtypeStruct((m, d), F32),
        compiler_params=pltpu.CompilerParams(dimension_semantics=("arbitrary",),
                                             vmem_limit_bytes=VMEM_LIMIT_BYTES),
        name=name,
    )(*args)


def _hyb_kernel(h_ref, g_ref, win_ref, wdt_ref, cw_ref, cb_ref, lng_ref, lnb_ref, scw_ref, scb_ref,
                dtb_ref, alog_ref, dskip_ref, snorm_ref, wout_ref, o_ref,
                ubuf, ycv, xbuf, state, *, ts):
    d = h_ref.shape[1]
    conv_ch = ycv.shape[0] * LANES
    inner = SSM_HEADS * SSM_HEAD_DIM
    gw = inner // SSM_GROUPS
    gs = SSM_GROUPS * SSM_STATE
    n_cb = conv_ch // LANES

    @pl.when(pl.program_id(1) == 0)
    def _():
        ubuf[:, 0:CONV_HALO, :] = jnp.zeros((n_cb, CONV_HALO, LANES), F32)
        xbuf[0:SSM_HALO, :] = jnp.zeros((SSM_HALO, xbuf.shape[1]), F32)
        state[...] = jnp.zeros(state.shape, F32)

    h = h_ref[...]
    hn = _rms(h, g_ref[...]).astype(BF16)

    val = _dot(hn, win_ref[:, 0:conv_ch])
    gate = _dot(hn, win_ref[:, conv_ch:2 * conv_ch])
    u = val * jax.nn.sigmoid(gate)
    for j in range(n_cb):
        ubuf[j, CONV_HALO:CONV_HALO + ts, :] = u[:, j * LANES:(j + 1) * LANES]

    def conv_cols(j, carry):
        w = cw_ref[j]
        acc = jnp.broadcast_to(cb_ref[j], (ts, LANES))
        for k in range(CONV_WIDTH):
            off = CONV_HALO - (CONV_WIDTH - 1) + k
            acc = acc + ubuf[j, off:off + ts, :] * w[k:k + 1, :]
        ycv[j] = acc
        ubuf[j, 0:CONV_HALO, :] = ubuf[j, ts:ts + CONV_HALO, :]
        return carry

    lax.fori_loop(0, n_cb, conv_cols, 0)
    yc = jnp.concatenate([ycv[j] for j in range(n_cb)], axis=1)
    mu = jnp.mean(yc, axis=-1, keepdims=True)
    ycc = yc - mu
    var = jnp.mean(ycc * ycc, axis=-1, keepdims=True)
    u_out = _silu(ycc * lax.rsqrt(var + EPS) * lng_ref[...] + lnb_ref[...]).astype(BF16)

    z = _dot(hn, win_ref[:, 2 * conv_ch:2 * conv_ch + inner])
    xbc = _dot(hn, win_ref[:, 2 * conv_ch + inner:])
    xbuf[SSM_HALO:SSM_HALO + ts, :] = xbc
    acc = jnp.broadcast_to(scb_ref[...], xbc.shape)
    for k in range(SSM_CONV):
        off = SSM_HALO - (SSM_CONV - 1) + k
        acc = acc + xbuf[off:off + ts, :] * scw_ref[k:k + 1, :]
    xbuf[0:SSM_HALO, :] = xbuf[ts:ts + SSM_HALO, :]
    xbc = _silu(acc)
    xs = xbc[:, 0:inner]
    bm = xbc[:, inner:inner + gs]
    cm = xbc[:, inner + gs:]

    dt_raw = _dot(hn, wdt_ref[...]) + dtb_ref[...]
    dt = jnp.maximum(dt_raw, 0.0) + jnp.log1p(jnp.exp(-jnp.abs(dt_raw)))
    adt = dt * (-jnp.exp(alog_ref[...]))

    ri = lax.broadcasted_iota(jnp.int32, (SSM_CHUNK, SSM_CHUNK), 0)
    ci = lax.broadcasted_iota(jnp.int32, (SSM_CHUNK, SSM_CHUNK), 1)
    tri = ri >= ci
    tri_b = jnp.where(tri, 1.0, 0.0).astype(BF16)
    eh = lax.broadcasted_iota(jnp.int32, (LANES, inner), 0)
    el = lax.broadcasted_iota(jnp.int32, (LANES, inner), 1)
    in_head = jnp.logical_and(el >= eh * SSM_HEAD_DIM, el < (eh + 1) * SSM_HEAD_DIM)
    expand = jnp.where(in_head, 1.0, 0.0).astype(BF16)

    ys = []
    for c in range(ts // SSM_CHUNK):
        rows = slice(c * SSM_CHUNK, (c + 1) * SSM_CHUNK)
        dt_c = dt[rows]
        acs = sum(_dot(tri_b, p) for p in _split_bf16(adt[rows], 3))
        acs_t = acs.T
        last = acs[SSM_CHUNK - 1:SSM_CHUNK, :]
        eacs = jnp.exp(acs)
        stack = jnp.concatenate([dt_c, dt_c * jnp.exp(last - acs), eacs], axis=0)
        wide = sum(_dot(p, expand) for p in _split_bf16(stack, 2))
        dt_x = wide[0:SSM_CHUNK]
        w_end = wide[SSM_CHUNK:2 * SSM_CHUNK]
        sc_off = wide[2 * SSM_CHUNK:]
        xs_c = xs[rows]
        xdt = (xs_c * dt_x).astype(BF16)
        xend = (xs_c * w_end).astype(BF16)
        y_cols = []
        for g in range(SSM_GROUPS):
            b_g = bm[rows, g * SSM_STATE:(g + 1) * SSM_STATE]
            c_g = cm[rows, g * SSM_STATE:(g + 1) * SSM_STATE].astype(BF16)
            cb = _dot_nt(c_g, b_g.astype(BF16))
            st = state[g]
            y_off = _dot(c_g, st.astype(BF16)) * sc_off[:, g * gw:(g + 1) * gw]
            s_new = _dot(b_g.T.astype(BF16), xend[:, g * gw:(g + 1) * gw])
            state[g] = st * sc_off[SSM_CHUNK - 1:SSM_CHUNK, g * gw:(g + 1) * gw] + s_new
            lane = lax.broadcasted_iota(jnp.int32, (SSM_CHUNK, LANES), 1)
            low = lane < SSM_HEAD_DIM
            for pr in range(gw // LANES):
                col = g * gw + pr * LANES
                h0 = col // SSM_HEAD_DIM
                ws = []
                for hh in (h0, h0 + 1):
                    diff = acs[:, hh:hh + 1] - acs_t[hh:hh + 1, :]
                    ws.append((cb * jnp.exp(jnp.where(tri, diff, -jnp.inf))).astype(BF16))
                xcol = xdt[:, col:col + LANES]
                zero = jnp.zeros_like(xcol)
                rhs = jnp.concatenate([jnp.where(low, xcol, zero), jnp.where(low, zero, xcol)], axis=0)
                y_cols.append(_dot(jnp.concatenate(ws, axis=1), rhs)
                              + y_off[:, pr * LANES:(pr + 1) * LANES])
        ys.append(jnp.concatenate(y_cols, axis=1) + dskip_ref[...] * xs_c)
    y = jnp.concatenate(ys, axis=0) * _silu(z)
    yn = []
    for g in range(SSM_GROUPS):
        yg = y[:, g * gw:(g + 1) * gw]
        yn.append(yg * lax.rsqrt(jnp.mean(yg * yg, axis=-1, keepdims=True) + EPS))
    y_out = (jnp.concatenate(yn, axis=1) * snorm_ref[...]).astype(BF16)

    o_ref[...] = h + _dot(u_out, wout_ref[0:conv_ch, :]) + _dot(y_out, wout_ref[conv_ch:, :])


def _hybrid_mixer(h, bsz, norm_g, w_in, conv_w, conv_b, ln_g, ln_b, sc_w, sc_b, dt_bias, a_log, d_skip,
                  ssm_norm, w_out):
    m, d = h.shape
    ts = HYB_TS
    nt = m // bsz // ts
    conv_ch = conv_w.shape[1]
    xbc_w = sc_w.shape[1]
    inner = SSM_HEADS * SSM_HEAD_DIM
    n_main = w_in.shape[1] - SSM_HEADS
    pad_h = lambda v: jnp.pad(v.reshape(1, SSM_HEADS), ((0, 0), (0, LANES - SSM_HEADS)))
    w_dt = jnp.pad(w_in[:, n_main:], ((0, 0), (0, LANES - SSM_HEADS))).astype(BF16)
    cw = jnp.pad(conv_w, ((0, CONV_HALO - CONV_WIDTH), (0, 0)))
    cw = cw.reshape(CONV_HALO, conv_ch // LANES, LANES).transpose(1, 0, 2)
    consts = [norm_g.reshape(1, d), w_in[:, :n_main].astype(BF16), w_dt, cw,
              conv_b.reshape(conv_ch // LANES, 1, LANES), ln_g.reshape(1, conv_ch), ln_b.reshape(1, conv_ch),
              sc_w, sc_b.reshape(1, xbc_w), pad_h(dt_bias), pad_h(a_log),
              jnp.repeat(d_skip, SSM_HEAD_DIM).reshape(1, inner), ssm_norm.reshape(1, inner),
              w_out.astype(BF16)]
    row = lambda b, s: (b * nt + s, 0)
    return pl.pallas_call(
        functools.partial(_hyb_kernel, ts=ts),
        grid=(bsz, nt),
        in_specs=[pl.BlockSpec((ts, d), row)] + [_const_spec(c.shape) for c in consts],
        out_specs=pl.BlockSpec((ts, d), row),
        out_shape=jax.ShapeDtypeStruct((m, d), F32),
        scratch_shapes=[
            pltpu.VMEM((conv_ch // LANES, CONV_HALO + ts, LANES), F32),
            pltpu.VMEM((conv_ch // LANES, ts, LANES), F32),
            pltpu.VMEM((SSM_HALO + ts, xbc_w), F32),
            pltpu.VMEM((SSM_GROUPS, SSM_STATE, inner // SSM_GROUPS), F32),
        ],
        compiler_params=pltpu.CompilerParams(dimension_semantics=("arbitrary", "arbitrary"),
                                             vmem_limit_bytes=VMEM_LIMIT_BYTES),
        name="hybrid_mixer",
    )(h, *consts)


def _attn_kernel(sinks_ref, h_ref, g_ref, wqkv_ref, bqkv_ref, cos_ref, sin_ref, wo_ref, bo_ref, o_ref,
                 kk, vv, att, *, ts):
    nq = ATT_HEADS * ATT_HEAD_DIM
    nkv = ATT_KV_HEADS * ATT_HEAD_DIM
    nblk = ts // WINDOW
    first_tile = pl.program_id(1) == 0

    @pl.when(first_tile)
    def _():
        kk[:, :, 0:WINDOW, :] = jnp.zeros((ATT_KV_HEADS, 2, WINDOW, LANES), BF16)
        vv[:, :, 0:WINDOW, :] = jnp.zeros((ATT_KV_HEADS, 2, WINDOW, LANES), BF16)

    h = h_ref[...]
    hn = _rms(h, g_ref[...]).astype(BF16)
    qkv = _dot(hn, wqkv_ref[...]) + bqkv_ref[...]
    cos = cos_ref[...]
    sin = sin_ref[...]
    lane = lax.broadcasted_iota(jnp.int32, (ts, LANES), 1)
    first_half = jnp.bitwise_and(lane, ATT_HEAD_DIM - 1) < ATT_HEAD_DIM // 2
    low = lane < ATT_HEAD_DIM

    def rope(t):
        partner = jnp.where(first_half, pltpu.roll(t, LANES - ATT_HEAD_DIM // 2, 1),
                            pltpu.roll(t, ATT_HEAD_DIM // 2, 1))
        return t * cos + partner * sin

    scale = ATT_HEAD_DIM ** -0.5
    q = [(rope(qkv[:, j * LANES:(j + 1) * LANES]) * scale).astype(BF16) for j in range(nq // LANES)]

    zero = jnp.zeros((ts, LANES), F32)
    for jj in range(nkv // LANES):
        kcol = rope(qkv[:, nq + jj * LANES:nq + (jj + 1) * LANES])
        vcol = qkv[:, nq + nkv + jj * LANES:nq + nkv + (jj + 1) * LANES]
        for buf, col in ((kk, kcol), (vv, vcol)):
            swapped = pltpu.roll(col, ATT_HEAD_DIM, 1)
            for e in range(2):
                kh = 2 * jj + e
                in_low = col if e == 0 else swapped
                in_high = swapped if e == 0 else col
                buf[kh, 0, WINDOW:WINDOW + ts, :] = jnp.where(low, in_low, zero).astype(BF16)
                buf[kh, 1, WINDOW:WINDOW + ts, :] = jnp.where(low, zero, in_high).astype(BF16)

    qi = lax.broadcasted_iota(jnp.int32, (WINDOW, 2 * WINDOW), 0)
    kj = lax.broadcasted_iota(jnp.int32, (WINDOW, 2 * WINDOW), 1)
    rel = qi - kj + WINDOW
    allowed = jnp.logical_and(rel >= 0, rel < WINDOW)
    allowed0 = jnp.logical_and(allowed, kj >= WINDOW * first_tile.astype(jnp.int32))
    neg = jnp.full((WINDOW, 2 * WINDOW), -jnp.inf, F32)
    zer = jnp.zeros((WINDOW, 2 * WINDOW), F32)
    bias_any = jnp.where(allowed, zer, neg)
    bias_first = jnp.where(allowed0, zer, neg)
    top = lax.broadcasted_iota(jnp.int32, (2 * WINDOW, 1), 0) < WINDOW
    low_o = lax.broadcasted_iota(jnp.int32, (2 * WINDOW, LANES), 1) < ATT_HEAD_DIM

    for i in range(nblk):
        rows = slice(i * WINDOW, (i + 1) * WINDOW)
        b1 = bias_first if i == 0 else bias_any
        bias = jnp.concatenate([b1, b1], axis=0)
        for kh in range(ATT_KV_HEADS):
            keys = slice(i * WINDOW, (i + 2) * WINDOW)
            q2 = jnp.concatenate([q[2 * kh][rows], q[2 * kh + 1][rows]], axis=0)
            k2 = jnp.concatenate([kk[kh, 0, keys, :], kk[kh, 1, keys, :]], axis=0)
            v2 = jnp.concatenate([vv[kh, 0, keys, :], vv[kh, 1, keys, :]], axis=0)
            logits = _dot_nt(q2, k2)
            es, dens = [], []
            for slot in range(2):
                lg = logits[:, slot * 2 * WINDOW:(slot + 1) * 2 * WINDOW] + bias
                sink = jnp.where(top, sinks_ref[4 * kh + slot], sinks_ref[4 * kh + 2 + slot])
                mx = jnp.maximum(jnp.max(lg, axis=-1, keepdims=True), sink)
                e = jnp.exp(lg - mx)
                es.append(e.astype(BF16))
                dens.append(jnp.sum(e, axis=-1, keepdims=True) + jnp.exp(sink - mx))
            o2 = _dot(jnp.concatenate(es, axis=1), v2) / jnp.where(low_o, dens[0], dens[1])
            att[rows, 2 * kh * LANES:(2 * kh + 1) * LANES] = o2[0:WINDOW].astype(BF16)
            att[rows, (2 * kh + 1) * LANES:(2 * kh + 2) * LANES] = o2[WINDOW:].astype(BF16)

    for buf in (kk, vv):
        buf[:, :, 0:WINDOW, :] = buf[:, :, ts:ts + WINDOW, :]
    o_ref[...] = h + _dot(att[...], wo_ref[...]) + bo_ref[...]


def _attention_mixer(h, bsz, g, w_qkv, b_qkv, sinks, cos, sin, w_o, b_o):
    m, d = h.shape
    ts = ATT_TS
    nt = m // bsz // ts
    row = lambda b, s: (b * nt + s, 0)
    tab = lambda b, s: (s, 0)
    consts = [g, w_qkv, b_qkv]
    return pl.pallas_call(
        functools.partial(_attn_kernel, ts=ts),
        grid=(bsz, nt),
        in_specs=[pl.BlockSpec(memory_space=pltpu.SMEM), pl.BlockSpec((ts, d), row)]
                 + [_const_spec(c.shape) for c in consts]
                 + [pl.BlockSpec((ts, LANES), tab), pl.BlockSpec((ts, LANES), tab),
                    _const_spec(w_o.shape), _const_spec(b_o.shape)],
        out_specs=pl.BlockSpec((ts, d), row),
        out_shape=jax.ShapeDtypeStruct((m, d), F32),
        scratch_shapes=[
            pltpu.VMEM((ATT_KV_HEADS, 2, WINDOW + ts, LANES), BF16),
            pltpu.VMEM((ATT_KV_HEADS, 2, WINDOW + ts, LANES), BF16),
            pltpu.VMEM((ts, ATT_HEADS * ATT_HEAD_DIM), BF16),
        ],
        compiler_params=pltpu.CompilerParams(dimension_semantics=("arbitrary", "arbitrary"),
                                             vmem_limit_bytes=VMEM_LIMIT_BYTES),
        name="attention_mixer",
    )(sinks, h, *consts, cos, sin, w_o, b_o)


def _rope_tables(seqlen):
    half = ATT_HEAD_DIM // 2
    inv = ROPE_THETA ** (-jnp.arange(0, ATT_HEAD_DIM, 2, dtype=F32) / ATT_HEAD_DIM)
    ang = jnp.arange(seqlen, dtype=F32)[:, None] * inv[None, :]
    cos, sin = jnp.cos(ang), jnp.sin(ang)
    reps = LANES // ATT_HEAD_DIM
    return (jnp.tile(jnp.concatenate([cos, cos], axis=1), (1, reps)),
            jnp.tile(jnp.concatenate([-sin, sin], axis=1), (1, reps)))


def kernel(x, p, norm_ffn1, ffn1_w_in, ffn1_w_out, norm_mix, norm_ffn2, ffn2_w_in, ffn2_w_out, ple_norm, ple_gate_w, ple_proj_w, hyb_w_in, conv_dw_w, conv_dw_b, conv_ln_g, conv_ln_b, ssm_conv_w, ssm_conv_b, ssm_dt_bias, ssm_a_log, ssm_d, ssm_norm, hyb_w_out, att_w_qkv, att_b_qkv, att_sinks, att_w_o, att_b_o, final_norm):
    bsz, seqlen, d = x.shape
    depth = p.shape[0]
    m = bsz * seqlen
    h = x.reshape(m, d)
    bf = lambda w: w.astype(BF16)
    cos, sin = _rope_tables(seqlen)

    for i in range(depth):
        j = i // 2
        h = _ffn(h, norm_ffn1[i], bf(ffn1_w_in[i]), bf(ffn1_w_out[i]), name=f"ffn1_l{i}")
        if i % 2 == 0:
            h = _hybrid_mixer(h, bsz, norm_mix[i], hyb_w_in[j], conv_dw_w[j], conv_dw_b[j], conv_ln_g[j],
                              conv_ln_b[j], ssm_conv_w[j], ssm_conv_b[j], ssm_dt_bias[j], ssm_a_log[j],
                              ssm_d[j], ssm_norm[j], hyb_w_out[j])
        else:
            h = _attention_mixer(h, bsz, norm_mix[i].reshape(1, d), bf(att_w_qkv[j]),
                                 att_b_qkv[j].reshape(1, -1), att_sinks[j], cos, sin,
                                 bf(att_w_o[j]), att_b_o[j].reshape(1, d))
        h = _ffn(h, norm_ffn2[i], bf(ffn2_w_in[i]), bf(ffn2_w_out[i]),
                 ple=(p[i].reshape(m, -1), ple_norm[i], bf(ple_gate_w[i]), bf(ple_proj_w[i])),
                 final_g=final_norm if i == depth - 1 else None, name=f"ffn2_l{i}")
    return h.reshape(bsz, seqlen, d)
```

```python
import functools

import jax
import jax.numpy as jnp
from jax import lax
from jax.experimental import pallas as pl
from jax.experimental.pallas import tpu as pltpu

F32 = jnp.float32
BF16 = jnp.bfloat16

EPS = 1e-6
LANES = 128
SUBLANES = 8
VMEM_LIMIT_BYTES = 56 * 1024 * 1024

CONV_WIDTH = 31
CONV_HALO = 32
SSM_HEADS = 16
SSM_HEAD_DIM = 64
SSM_GROUPS = 2
SSM_STATE = 128
SSM_CONV = 4
SSM_HALO = 8
SSM_CHUNK = 128
ATT_HEADS = 16
ATT_KV_HEADS = 4
ATT_HEAD_DIM = 64
WINDOW = 128
ROPE_THETA = 10000.0

FFN_TM = 512
FFN_FC = 256
HYB_TS = 256
CONV_ROWS = 256
PROJ_PIECE = 256
ATT_TS = 512


def _rms(x, g):
    return x * lax.rsqrt(jnp.mean(x * x, axis=-1, keepdims=True) + EPS) * g


def _silu(x):
    return x * jax.nn.sigmoid(x)


def _dot(a, b):
    return jnp.dot(a, b, preferred_element_type=F32)


def _dot_nt(a, b):
    return lax.dot_general(a, b, (((1,), (1,)), ((), ())), preferred_element_type=F32)


def _const_spec(shape):
    nd = len(shape)
    return pl.BlockSpec(shape, lambda *_: (0,) * nd, pipeline_mode=pl.Buffered(1))


def _split_bf16(x, parts):
    out = []
    r = x
    for _ in range(parts):
        p = r.astype(BF16)
        out.append(p)
        r = r - p.astype(F32)
    return out


def _ffn_kernel(*refs, d_ff, fc, with_ple, with_final):
    h_ref, g_ref, win_ref, wout_ref = refs[:4]
    rest = list(refs[4:])
    if with_ple:
        p_ref, pn_ref, pgw_ref, ppw_ref = rest[:4]
        rest = rest[4:]
    if with_final:
        fn_ref = rest[0]
        rest = rest[1:]
    (o_ref,) = rest

    h = h_ref[...]
    xn = _rms(h, g_ref[...]).astype(BF16)
    acc = jnp.zeros(h.shape, F32)
    for c in range(d_ff // fc):
        gate = _dot(xn, win_ref[:, c * fc:(c + 1) * fc])
        up = _dot(xn, win_ref[:, d_ff + c * fc:d_ff + (c + 1) * fc])
        act = (_silu(gate) * up).astype(BF16)
        acc = acc + _dot(act, wout_ref[c * fc:(c + 1) * fc, :])
    h = h + 0.5 * acc
    if with_ple:
        gate = jax.nn.sigmoid(_dot(_rms(h, pn_ref[...]).astype(BF16), pgw_ref[...]))
        h = h + gate * _dot(p_ref[...].astype(BF16), ppw_ref[...])
    if with_final:
        h = _rms(h, fn_ref[...])
    o_ref[...] = h


def _ffn(h, g, w_in, w_out, ple=None, final_g=None, *, name):
    m, d = h.shape
    d_ff = w_out.shape[0]
    tm = FFN_TM
    row = lambda i: (i, 0)
    in_specs = [pl.BlockSpec((tm, d), row), _const_spec((1, d)),
                _const_spec(w_in.shape), _const_spec(w_out.shape)]
    args = [h, g.reshape(1, d), w_in, w_out]
    if ple is not None:
        p, pn, pgw, ppw = ple
        in_specs += [pl.BlockSpec((tm, p.shape[1]), row), _const_spec((1, d)),
                     _const_spec(pgw.shape), _const_spec(ppw.shape)]
        args += [p, pn.reshape(1, d), pgw, ppw]
    if final_g is not None:
        in_specs.append(_const_spec((1, d)))
        args.append(final_g.reshape(1, d))
    return pl.pallas_call(
        functools.partial(_ffn_kernel, d_ff=d_ff, fc=FFN_FC, with_ple=ple is not None,
                          with_final=final_g is not None),
        grid=(m // tm,),
        in_specs=in_specs,
        out_specs=pl.BlockSpec((tm, d), row),
        out_shape=jax.ShapeDtypeStruct((m, d), F32),
        compiler_params=pltpu.CompilerParams(dimension_semantics=("arbitrary",),
                                             vmem_limit_bytes=VMEM_LIMIT_BYTES),
        name=name,
    )(*args)


def _causal_conv_blocks(buf, out, w_ref, b_ref, *, taps, halo, ts, feeders, act=None):
    n_blocks = buf.shape[0]
    per_feeder = n_blocks // len(feeders)
    for j in range(n_blocks):
        if j % per_feeder == 0:
            feeders[j // per_feeder]()
        w = w_ref[j]
        for r0 in range(0, ts, CONV_ROWS):
            acc = jnp.broadcast_to(b_ref[j], (CONV_ROWS, LANES))
            for k in range(taps):
                off = halo - (taps - 1) + k + r0
                acc = acc + buf[j, off:off + CONV_ROWS, :] * w[k:k + 1, :]
            out[j, r0:r0 + CONV_ROWS, :] = acc if act is None else act(acc)
        buf[j, 0:halo, :] = buf[j, ts:ts + halo, :]


def _hyb_kernel(h_ref, g_ref, win_ref, wdt_ref, cw_ref, cb_ref, lng_ref, lnb_ref, scw_ref, scb_ref,
                dtb_ref, alog_ref, dskip_ref, snorm_ref, wout_ref, o_ref,
                ubuf, ycv, xbuf, xcv, state, *, ts):
    n_cb = ycv.shape[0]
    n_xb = xcv.shape[0]
    conv_ch = n_cb * LANES
    inner = SSM_HEADS * SSM_HEAD_DIM
    gw = inner // SSM_GROUPS
    n_ib = inner // LANES
    n_sb = SSM_STATE // LANES
    pw = PROJ_PIECE
    bpp = pw // LANES

    @pl.when(pl.program_id(1) == 0)
    def _():
        ubuf[:, 0:CONV_HALO, :] = jnp.zeros((n_cb, CONV_HALO, LANES), F32)
        xbuf[:, 0:SSM_HALO, :] = jnp.zeros((n_xb, SSM_HALO, LANES), F32)
        state[...] = jnp.zeros(state.shape, F32)

    h = h_ref[...]
    hn = _rms(h, g_ref[...]).astype(BF16)

    def glu(c):
        val = _dot(hn, win_ref[:, c * pw:(c + 1) * pw])
        gate = _dot(hn, win_ref[:, conv_ch + c * pw:conv_ch + (c + 1) * pw])
        u = val * jax.nn.sigmoid(gate)
        for jj in range(bpp):
            ubuf[c * bpp + jj, CONV_HALO:CONV_HALO + ts, :] = u[:, jj * LANES:(jj + 1) * LANES]

    def xbc_cols(c):
        base = 2 * conv_ch + inner
        x = _dot(hn, win_ref[:, base + c * pw:base + (c + 1) * pw])
        for jj in range(bpp):
            xbuf[c * bpp + jj, SSM_HALO:SSM_HALO + ts, :] = x[:, jj * LANES:(jj + 1) * LANES]

    _causal_conv_blocks(ubuf, ycv, cw_ref, cb_ref, taps=CONV_WIDTH, halo=CONV_HALO, ts=ts,
                        feeders=[functools.partial(glu, c) for c in range(n_cb // bpp)])
    yc = jnp.concatenate([ycv[j] for j in range(n_cb)], axis=1)
    mu = jnp.mean(yc, axis=-1, keepdims=True)
    ycc = yc - mu
    var = jnp.mean(ycc * ycc, axis=-1, keepdims=True)
    u_out = _silu(ycc * lax.rsqrt(var + EPS) * lng_ref[...] + lnb_ref[...]).astype(BF16)

    _causal_conv_blocks(xbuf, xcv, scw_ref, scb_ref, taps=SSM_CONV, halo=SSM_HALO, ts=ts, act=_silu,
                        feeders=[functools.partial(xbc_cols, c) for c in range(n_xb // bpp)])
    z = _dot(hn, win_ref[:, 2 * conv_ch:2 * conv_ch + inner])
    dt_raw = _dot(hn, wdt_ref[...]) + dtb_ref[...]
    dt = jnp.maximum(dt_raw, 0.0) + jnp.log1p(jnp.exp(-jnp.abs(dt_raw)))
    adt = dt * (-jnp.exp(alog_ref[...]))

    ri = lax.broadcasted_iota(jnp.int32, (SSM_CHUNK, SSM_CHUNK), 0)
    ci = lax.broadcasted_iota(jnp.int32, (SSM_CHUNK, SSM_CHUNK), 1)
    tri = ri >= ci
    tri_b = jnp.where(tri, 1.0, 0.0).astype(BF16)
    eh = lax.broadcasted_iota(jnp.int32, (LANES, inner), 0)
    el = lax.broadcasted_iota(jnp.int32, (LANES, inner), 1)
    in_head = jnp.logical_and(el >= eh * SSM_HEAD_DIM, el < (eh + 1) * SSM_HEAD_DIM)
    expand = jnp.where(in_head, 1.0, 0.0).astype(BF16)

    ys = []
    for c in range(ts // SSM_CHUNK):
        rows = slice(c * SSM_CHUNK, (c + 1) * SSM_CHUNK)
        dt_c = dt[rows]
        acs = sum(_dot(tri_b, p) for p in _split_bf16(adt[rows], 3))
        acs_t = acs.T
        last = acs[SSM_CHUNK - 1:SSM_CHUNK, :]
        stack = jnp.concatenate([dt_c, dt_c * jnp.exp(last - acs)], axis=0).astype(BF16)
        wide = _dot(stack, expand)
        dt_x = wide[0:SSM_CHUNK]
        w_end = wide[SSM_CHUNK:]
        sc_off = sum(_dot(p, expand) for p in _split_bf16(jnp.exp(acs), 2))
        xs_c = jnp.concatenate([xcv[j, rows, :] for j in range(n_ib)], axis=1)
        xdt = (xs_c * dt_x).astype(BF16)
        xend = (xs_c * w_end).astype(BF16)
        y_cols = []
        for g in range(SSM_GROUPS):
            b_g = jnp.concatenate([xcv[n_ib + g * n_sb + j, rows, :] for j in range(n_sb)], axis=1)
            c_g = jnp.concatenate([xcv[n_ib + (SSM_GROUPS + g) * n_sb + j, rows, :] for j in range(n_sb)],
                                  axis=1).astype(BF16)
            cb = _dot_nt(c_g, b_g.astype(BF16))
            st = state[g]
            y_off = _dot(c_g, st.astype(BF16)) * sc_off[:, g * gw:(g + 1) * gw]
            s_new = _dot(b_g.T.astype(BF16), xend[:, g * gw:(g + 1) * gw])
            state[g] = st * sc_off[SSM_CHUNK - 1:SSM_CHUNK, g * gw:(g + 1) * gw] + s_new
            lane = lax.broadcasted_iota(jnp.int32, (SSM_CHUNK, LANES), 1)
            low = lane < SSM_HEAD_DIM
            for pr in range(gw // LANES):
                col = g * gw + pr * LANES
                h0 = col // SSM_HEAD_DIM
                ws = []
                for hh in (h0, h0 + 1):
                    diff = acs[:, hh:hh + 1] - acs_t[hh:hh + 1, :]
                    ws.append((cb * jnp.exp(jnp.where(tri, diff, -jnp.inf))).astype(BF16))
                xcol = xdt[:, col:col + LANES]
                zero = jnp.zeros_like(xcol)
                rhs = jnp.concatenate([jnp.where(low, xcol, zero), jnp.where(low, zero, xcol)], axis=0)
                y_cols.append(_dot(jnp.concatenate(ws, axis=1), rhs)
                              + y_off[:, pr * LANES:(pr + 1) * LANES])
        ys.append(jnp.concatenate(y_cols, axis=1) + dskip_ref[...] * xs_c)
    y = jnp.concatenate(ys, axis=0) * _silu(z)
    yn = []
    for g in range(SSM_GROUPS):
        yg = y[:, g * gw:(g + 1) * gw]
        yn.append(yg * lax.rsqrt(jnp.mean(yg * yg, axis=-1, keepdims=True) + EPS))
    y_out = (jnp.concatenate(yn, axis=1) * snorm_ref[...]).astype(BF16)

    o_ref[...] = h + _dot(u_out, wout_ref[0:conv_ch, :]) + _dot(y_out, wout_ref[conv_ch:, :])


def _hybrid_mixer(h, bsz, norm_g, w_in, conv_w, conv_b, ln_g, ln_b, sc_w, sc_b, dt_bias, a_log, d_skip,
                  ssm_norm, w_out):
    m, d = h.shape
    ts = HYB_TS
    nt = m // bsz // ts
    conv_ch = conv_w.shape[1]
    xbc_w = sc_w.shape[1]
    inner = SSM_HEADS * SSM_HEAD_DIM
    n_main = w_in.shape[1] - SSM_HEADS
    pad_h = lambda v: jnp.pad(v.reshape(1, SSM_HEADS), ((0, 0), (0, LANES - SSM_HEADS)))
    w_dt = jnp.pad(w_in[:, n_main:], ((0, 0), (0, LANES - SSM_HEADS))).astype(BF16)
    taps_by_block = lambda w, rows: jnp.pad(w, ((0, rows - w.shape[0]), (0, 0))).reshape(
        rows, w.shape[1] // LANES, LANES).transpose(1, 0, 2)
    consts = [norm_g.reshape(1, d), w_in[:, :n_main].astype(BF16), w_dt, taps_by_block(conv_w, CONV_HALO),
              conv_b.reshape(conv_ch // LANES, 1, LANES), ln_g.reshape(1, conv_ch), ln_b.reshape(1, conv_ch),
              taps_by_block(sc_w, SSM_HALO), sc_b.reshape(xbc_w // LANES, 1, LANES), pad_h(dt_bias), pad_h(a_log),
              jnp.repeat(d_skip, SSM_HEAD_DIM).reshape(1, inner), ssm_norm.reshape(1, inner),
              w_out.astype(BF16)]
    row = lambda b, s: (b * nt + s, 0)
    return pl.pallas_call(
        functools.partial(_hyb_kernel, ts=ts),
        grid=(bsz, nt),
        in_specs=[pl.BlockSpec((ts, d), row)] + [_const_spec(c.shape) for c in consts],
        out_specs=pl.BlockSpec((ts, d), row),
        out_shape=jax.ShapeDtypeStruct((m, d), F32),
        scratch_shapes=[
            pltpu.VMEM((conv_ch // LANES, CONV_HALO + ts, LANES), F32),
            pltpu.VMEM((conv_ch // LANES, ts, LANES), F32),
            pltpu.VMEM((xbc_w // LANES, SSM_HALO + ts, LANES), F32),
            pltpu.VMEM((xbc_w // LANES, ts, LANES), F32),
            pltpu.VMEM((SSM_GROUPS, SSM_STATE, inner // SSM_GROUPS), F32),
        ],
        compiler_params=pltpu.CompilerParams(dimension_semantics=("arbitrary", "arbitrary"),
                                             vmem_limit_bytes=VMEM_LIMIT_BYTES),
        name="hybrid_mixer",
    )(h, *consts)


def _attn_kernel(sinks_ref, h_ref, g_ref, wqkv_ref, bqkv_ref, cos_ref, sin_ref, wo_ref, bo_ref, o_ref,
                 kk, vv, att, *, ts):
    nq = ATT_HEADS * ATT_HEAD_DIM
    nkv = ATT_KV_HEADS * ATT_HEAD_DIM
    nblk = ts // WINDOW
    first_tile = pl.program_id(1) == 0

    @pl.when(first_tile)
    def _():
        kk[:, :, 0:WINDOW, :] = jnp.zeros((ATT_KV_HEADS, 2, WINDOW, LANES), BF16)
        vv[:, :, 0:WINDOW, :] = jnp.zeros((ATT_KV_HEADS, 2, WINDOW, LANES), BF16)

    h = h_ref[...]
    hn = _rms(h, g_ref[...]).astype(BF16)
    qkv = _dot(hn, wqkv_ref[...]) + bqkv_ref[...]
    cos = cos_ref[...]
    sin = sin_ref[...]
    lane = lax.broadcasted_iota(jnp.int32, (ts, LANES), 1)
    first_half = jnp.bitwise_and(lane, ATT_HEAD_DIM - 1) < ATT_HEAD_DIM // 2
    low = lane < ATT_HEAD_DIM

    def rope(t):
        partner = jnp.where(first_half, pltpu.roll(t, LANES - ATT_HEAD_DIM // 2, 1),
                            pltpu.roll(t, ATT_HEAD_DIM // 2, 1))
        return t * cos + partner * sin

    scale = ATT_HEAD_DIM ** -0.5
    q = [(rope(qkv[:, j * LANES:(j + 1) * LANES]) * scale).astype(BF16) for j in range(nq // LANES)]

    zero = jnp.zeros((ts, LANES), F32)
    for jj in range(nkv // LANES):
        kcol = rope(qkv[:, nq + jj * LANES:nq + (jj + 1) * LANES])
        vcol = qkv[:, nq + nkv + jj * LANES:nq + nkv + (jj + 1) * LANES]
        for buf, col in ((kk, kcol), (vv, vcol)):
            swapped = pltpu.roll(col, ATT_HEAD_DIM, 1)
            for e in range(2):
                kh = 2 * jj + e
                in_low = col if e == 0 else swapped
                in_high = swapped if e == 0 else col
                buf[kh, 0, WINDOW:WINDOW + ts, :] = jnp.where(low, in_low, zero).astype(BF16)
                buf[kh, 1, WINDOW:WINDOW + ts, :] = jnp.where(low, zero, in_high).astype(BF16)

    qi = lax.broadcasted_iota(jnp.int32, (WINDOW, 2 * WINDOW), 0)
    kj = lax.broadcasted_iota(jnp.int32, (WINDOW, 2 * WINDOW), 1)
    rel = qi - kj + WINDOW
    allowed = jnp.logical_and(rel >= 0, rel < WINDOW)
    allowed0 = jnp.logical_and(allowed, kj >= WINDOW * first_tile.astype(jnp.int32))
    neg = jnp.full((WINDOW, 2 * WINDOW), -jnp.inf, F32)
    zer = jnp.zeros((WINDOW, 2 * WINDOW), F32)
    bias_any = jnp.where(allowed, zer, neg)
    bias_first = jnp.where(allowed0, zer, neg)
    top = lax.broadcasted_iota(jnp.int32, (2 * WINDOW, 1), 0) < WINDOW
    low_o = lax.broadcasted_iota(jnp.int32, (2 * WINDOW, LANES), 1) < ATT_HEAD_DIM

    for i in range(nblk):
        rows = slice(i * WINDOW, (i + 1) * WINDOW)
        b1 = bias_first if i == 0 else bias_any
        bias = jnp.concatenate([b1, b1], axis=0)
        for kh in range(ATT_KV_HEADS):
            keys = slice(i * WINDOW, (i + 2) * WINDOW)
            q2 = jnp.concatenate([q[2 * kh][rows], q[2 * kh + 1][rows]], axis=0)
            k2 = jnp.concatenate([kk[kh, 0, keys, :], kk[kh, 1, keys, :]], axis=0)
            v2 = jnp.concatenate([vv[kh, 0, keys, :], vv[kh, 1, keys, :]], axis=0)
            logits = _dot_nt(q2, k2)
            es, dens = [], []
            for slot in range(2):
                lg = logits[:, slot * 2 * WINDOW:(slot + 1) * 2 * WINDOW] + bias
                sink = jnp.where(top, sinks_ref[4 * kh + slot], sinks_ref[4 * kh + 2 + slot])
                mx = jnp.maximum(jnp.max(lg, axis=-1, keepdims=True), sink)
                e = jnp.exp(lg - mx)
                es.append(e.astype(BF16))
                dens.append(jnp.sum(e, axis=-1, keepdims=True) + jnp.exp(sink - mx))
            o2 = _dot(jnp.concatenate(es, axis=1), v2) / jnp.where(low_o, dens[0], dens[1])
            att[rows, 2 * kh * LANES:(2 * kh + 1) * LANES] = o2[0:WINDOW].astype(BF16)
            att[rows, (2 * kh + 1) * LANES:(2 * kh + 2) * LANES] = o2[WINDOW:].astype(BF16)

    for buf in (kk, vv):
        buf[:, :, 0:WINDOW, :] = buf[:, :, ts:ts + WINDOW, :]
    o_ref[...] = h + _dot(att[...], wo_ref[...]) + bo_ref[...]


def _attention_mixer(h, bsz, g, w_qkv, b_qkv, sinks, cos, sin, w_o, b_o):
    m, d = h.shape
    ts = ATT_TS
    nt = m // bsz // ts
    row = lambda b, s: (b * nt + s, 0)
    tab = lambda b, s: (s, 0)
    consts = [g, w_qkv, b_qkv]
    return pl.pallas_call(
        functools.partial(_attn_kernel, ts=ts),
        grid=(bsz, nt),
        in_specs=[pl.BlockSpec(memory_space=pltpu.SMEM), pl.BlockSpec((ts, d), row)]
                 + [_const_spec(c.shape) for c in consts]
                 + [pl.BlockSpec((ts, LANES), tab), pl.BlockSpec((ts, LANES), tab),
                    _const_spec(w_o.shape), _const_spec(b_o.shape)],
        out_specs=pl.BlockSpec((ts, d), row),
        out_shape=jax.ShapeDtypeStruct((m, d), F32),
        scratch_shapes=[
            pltpu.VMEM((ATT_KV_HEADS, 2, WINDOW + ts, LANES), BF16),
            pltpu.VMEM((ATT_KV_HEADS, 2, WINDOW + ts, LANES), BF16),
            pltpu.VMEM((ts, ATT_HEADS * ATT_HEAD_DIM), BF16),
        ],
        compiler_params=pltpu.CompilerParams(dimension_semantics=("arbitrary", "arbitrary"),
                                             vmem_limit_bytes=VMEM_LIMIT_BYTES),
        name="attention_mixer",
    )(sinks, h, *consts, cos, sin, w_o, b_o)


def _rope_tables(seqlen):
    inv = ROPE_THETA ** (-jnp.arange(0, ATT_HEAD_DIM, 2, dtype=F32) / ATT_HEAD_DIM)
    ang = jnp.arange(seqlen, dtype=F32)[:, None] * inv[None, :]
    cos, sin = jnp.cos(ang), jnp.sin(ang)
    reps = LANES // ATT_HEAD_DIM
    return (jnp.tile(jnp.concatenate([cos, cos], axis=1), (1, reps)),
            jnp.tile(jnp.concatenate([-sin, sin], axis=1), (1, reps)))


def kernel(x, p, norm_ffn1, ffn1_w_in, ffn1_w_out, norm_mix, norm_ffn2, ffn2_w_in, ffn2_w_out, ple_norm, ple_gate_w, ple_proj_w, hyb_w_in, conv_dw_w, conv_dw_b, conv_ln_g, conv_ln_b, ssm_conv_w, ssm_conv_b, ssm_dt_bias, ssm_a_log, ssm_d, ssm_norm, hyb_w_out, att_w_qkv, att_b_qkv, att_sinks, att_w_o, att_b_o, final_norm):
    bsz, seqlen, d = x.shape
    depth = p.shape[0]
    m = bsz * seqlen
    h = x.reshape(m, d)
    bf = lambda w: w.astype(BF16)
    cos, sin = _rope_tables(seqlen)

    for i in range(depth):
        j = i // 2
        h = _ffn(h, norm_ffn1[i], bf(ffn1_w_in[i]), bf(ffn1_w_out[i]), name=f"ffn1_l{i}")
        if i % 2 == 0:
            h = _hybrid_mixer(h, bsz, norm_mix[i], hyb_w_in[j], conv_dw_w[j], conv_dw_b[j], conv_ln_g[j],
                              conv_ln_b[j], ssm_conv_w[j], ssm_conv_b[j], ssm_dt_bias[j], ssm_a_log[j],
                              ssm_d[j], ssm_norm[j], hyb_w_out[j])
        else:
            h = _attention_mixer(h, bsz, norm_mix[i].reshape(1, d), bf(att_w_qkv[j]),
                                 att_b_qkv[j].reshape(1, -1), att_sinks[j], cos, sin,
                                 bf(att_w_o[j]), att_b_o[j].reshape(1, d))
        h = _ffn(h, norm_ffn2[i], bf(ffn2_w_in[i]), bf(ffn2_w_out[i]),
                 ple=(p[i].reshape(m, -1), ple_norm[i], bf(ple_gate_w[i]), bf(ple_proj_w[i])),
                 final_g=final_norm if i == depth - 1 else None, name=f"ffn2_l{i}")
    return h.reshape(bsz, seqlen, d)
```

```python
import functools

import jax
import jax.numpy as jnp
from jax import lax
from jax.experimental import pallas as pl
from jax.experimental.pallas import tpu as pltpu

F32 = jnp.float32
BF16 = jnp.bfloat16

EPS = 1e-6
LANES = 128
SUBLANES = 8
VMEM_LIMIT_BYTES = 56 * 1024 * 1024

CONV_WIDTH = 31
CONV_HALO = 32
SSM_HEADS = 16
SSM_HEAD_DIM = 64
SSM_GROUPS = 2
SSM_STATE = 128
SSM_CONV = 4
SSM_HALO = 8
SSM_CHUNK = 128
ATT_HEADS = 16
ATT_KV_HEADS = 4
ATT_HEAD_DIM = 64
WINDOW = 128
ROPE_THETA = 10000.0

FFN_TM = 512
FFN_FC = 256
HYB_TS = 256
CONV_ROWS = 256
PROJ_PIECE = 256
ATT_TS = 512


def _rms(x, g):
    return x * lax.rsqrt(jnp.mean(x * x, axis=-1, keepdims=True) + EPS) * g


def _silu(x):
    return x * jax.nn.sigmoid(x)


def _dot(a, b):
    return jnp.dot(a, b, preferred_element_type=F32)


def _dot_nt(a, b):
    return lax.dot_general(a, b, (((1,), (1,)), ((), ())), preferred_element_type=F32)


def _const_spec(shape):
    nd = len(shape)
    return pl.BlockSpec(shape, lambda *_: (0,) * nd, pipeline_mode=pl.Buffered(1))


def _split_bf16(x, parts):
    out = []
    r = x
    for _ in range(parts):
        p = r.astype(BF16)
        out.append(p)
        r = r - p.astype(F32)
    return out


def _ffn_kernel(*refs, d_ff, fc, with_ple, with_final):
    h_ref, g_ref, win_ref, wout_ref = refs[:4]
    rest = list(refs[4:])
    if with_ple:
        p_ref, pn_ref, pgw_ref, ppw_ref = rest[:4]
        rest = rest[4:]
    if with_final:
        fn_ref = rest[0]
        rest = rest[1:]
    (o_ref,) = rest

    h = h_ref[...]
    xn = _rms(h, g_ref[...]).astype(BF16)
    acc = jnp.zeros(h.shape, F32)
    for c in range(d_ff // fc):
        gate = _dot(xn, win_ref[:, c * fc:(c + 1) * fc])
        up = _dot(xn, win_ref[:, d_ff + c * fc:d_ff + (c + 1) * fc])
        act = (_silu(gate) * up).astype(BF16)
        acc = acc + _dot(act, wout_ref[c * fc:(c + 1) * fc, :])
    h = h + 0.5 * acc
    if with_ple:
        gate = jax.nn.sigmoid(_dot(_rms(h, pn_ref[...]).astype(BF16), pgw_ref[...]))
        h = h + gate * _dot(p_ref[...].astype(BF16), ppw_ref[...])
    if with_final:
        h = _rms(h, fn_ref[...])
    o_ref[...] = h


def _ffn(h, g, w_in, w_out, ple=None, final_g=None, *, name):
    m, d = h.shape
    d_ff = w_out.shape[0]
    tm = FFN_TM
    row = lambda i: (i, 0)
    in_specs = [pl.BlockSpec((tm, d), row), _const_spec((1, d)),
                _const_spec(w_in.shape), _const_spec(w_out.shape)]
    args = [h, g.reshape(1, d), w_in, w_out]
    if ple is not None:
        p, pn, pgw, ppw = ple
        in_specs += [pl.BlockSpec((tm, p.shape[1]), row), _const_spec((1, d)),
                     _const_spec(pgw.shape), _const_spec(ppw.shape)]
        args += [p, pn.reshape(1, d), pgw, ppw]
    if final_g is not None:
        in_specs.append(_const_spec((1, d)))
        args.append(final_g.reshape(1, d))
    return pl.pallas_call(
        functools.partial(_ffn_kernel, d_ff=d_ff, fc=FFN_FC, with_ple=ple is not None,
                          with_final=final_g is not None),
        grid=(m // tm,),
        in_specs=in_specs,
        out_specs=pl.BlockSpec((tm, d), row),
        out_shape=jax.ShapeDtypeStruct((m, d), F32),
        compiler_params=pltpu.CompilerParams(dimension_semantics=("arbitrary",),
                                             vmem_limit_bytes=VMEM_LIMIT_BYTES),
        name=name,
    )(*args)


def _causal_conv_blocks(buf, out, w_ref, b_ref, *, taps, halo, ts, feeders, act=None):
    n_blocks = buf.shape[0]
    per_feeder = n_blocks // len(feeders)
    for j in range(n_blocks):
        if j % per_feeder == 0:
            feeders[j // per_feeder]()
        w = w_ref[j]
        for r0 in range(0, ts, CONV_ROWS):
            acc = jnp.broadcast_to(b_ref[j], (CONV_ROWS, LANES))
            for k in range(taps):
                off = halo - (taps - 1) + k + r0
                acc = acc + buf[j, off:off + CONV_ROWS, :] * w[k:k + 1, :]
            out[j, r0:r0 + CONV_ROWS, :] = acc if act is None else act(acc)
        buf[j, 0:halo, :] = buf[j, ts:ts + halo, :]


def _hyb_kernel(h_ref, g_ref, win_ref, wdt_ref, cw_ref, cb_ref, lng_ref, lnb_ref, scw_ref, scb_ref,
                dtb_ref, alog_ref, dskip_ref, snorm_ref, wout_ref, o_ref,
                ubuf, ycv, xbuf, xcv, state, *, ts):
    n_cb = ycv.shape[0]
    n_xb = xcv.shape[0]
    conv_ch = n_cb * LANES
    inner = SSM_HEADS * SSM_HEAD_DIM
    gw = inner // SSM_GROUPS
    n_ib = inner // LANES
    n_sb = SSM_STATE // LANES
    pw = PROJ_PIECE
    bpp = pw // LANES

    @pl.when(pl.program_id(1) == 0)
    def _():
        ubuf[:, 0:CONV_HALO, :] = jnp.zeros((n_cb, CONV_HALO, LANES), F32)
        xbuf[:, 0:SSM_HALO, :] = jnp.zeros((n_xb, SSM_HALO, LANES), F32)
        state[...] = jnp.zeros(state.shape, F32)

    h = h_ref[...]
    hn = _rms(h, g_ref[...]).astype(BF16)

    def glu(c):
        val = _dot(hn, win_ref[:, c * pw:(c + 1) * pw])
        gate = _dot(hn, win_ref[:, conv_ch + c * pw:conv_ch + (c + 1) * pw])
        u = val * jax.nn.sigmoid(gate)
        for jj in range(bpp):
            ubuf[c * bpp + jj, CONV_HALO:CONV_HALO + ts, :] = u[:, jj * LANES:(jj + 1) * LANES]

    def xbc_cols(c):
        base = 2 * conv_ch + inner
        x = _dot(hn, win_ref[:, base + c * pw:base + (c + 1) * pw])
        for jj in range(bpp):
            xbuf[c * bpp + jj, SSM_HALO:SSM_HALO + ts, :] = x[:, jj * LANES:(jj + 1) * LANES]

    _causal_conv_blocks(ubuf, ycv, cw_ref, cb_ref, taps=CONV_WIDTH, halo=CONV_HALO, ts=ts,
                        feeders=[functools.partial(glu, c) for c in range(n_cb // bpp)])
    yc = jnp.concatenate([ycv[j] for j in range(n_cb)], axis=1)
    mu = jnp.mean(yc, axis=-1, keepdims=True)
    ycc = yc - mu
    var = jnp.mean(ycc * ycc, axis=-1, keepdims=True)
    u_out = _silu(ycc * lax.rsqrt(var + EPS) * lng_ref[...] + lnb_ref[...]).astype(BF16)

    _causal_conv_blocks(xbuf, xcv, scw_ref, scb_ref, taps=SSM_CONV, halo=SSM_HALO, ts=ts, act=_silu,
                        feeders=[functools.partial(xbc_cols, c) for c in range(n_xb // bpp)])
    z = _dot(hn, win_ref[:, 2 * conv_ch:2 * conv_ch + inner])
    dt_raw = _dot(hn, wdt_ref[...]) + dtb_ref[...]
    dt = jnp.maximum(dt_raw, 0.0) + jnp.log1p(jnp.exp(-jnp.abs(dt_raw)))
    adt = dt * (-jnp.exp(alog_ref[...]))

    ri = lax.broadcasted_iota(jnp.int32, (SSM_CHUNK, SSM_CHUNK), 0)
    ci = lax.broadcasted_iota(jnp.int32, (SSM_CHUNK, SSM_CHUNK), 1)
    tri = ri >= ci
    tri_b = jnp.where(tri, 1.0, 0.0).astype(BF16)
    eh = lax.broadcasted_iota(jnp.int32, (LANES, inner), 0)
    el = lax.broadcasted_iota(jnp.int32, (LANES, inner), 1)
    in_head = jnp.logical_and(el >= eh * SSM_HEAD_DIM, el < (eh + 1) * SSM_HEAD_DIM)
    expand = jnp.where(in_head, 1.0, 0.0).astype(BF16)

    ys = []
    for c in range(ts // SSM_CHUNK):
        rows = slice(c * SSM_CHUNK, (c + 1) * SSM_CHUNK)
        dt_c = dt[rows]
        acs = sum(_dot(tri_b, p) for p in _split_bf16(adt[rows], 3))
        acs_t = acs.T
        last = acs[SSM_CHUNK - 1:SSM_CHUNK, :]
        stack = jnp.concatenate([dt_c, dt_c * jnp.exp(last - acs)], axis=0).astype(BF16)
        wide = _dot(stack, expand)
        dt_x = wide[0:SSM_CHUNK]
        w_end = wide[SSM_CHUNK:]
        sc_off = sum(_dot(p, expand) for p in _split_bf16(jnp.exp(acs), 2))
        xs_c = jnp.concatenate([xcv[j, rows, :] for j in range(n_ib)], axis=1)
        xdt = (xs_c * dt_x).astype(BF16)
        xend = (xs_c * w_end).astype(BF16)
        y_cols = []
        for g in range(SSM_GROUPS):
            b_g = jnp.concatenate([xcv[n_ib + g * n_sb + j, rows, :] for j in range(n_sb)], axis=1)
            c_g = jnp.concatenate([xcv[n_ib + (SSM_GROUPS + g) * n_sb + j, rows, :] for j in range(n_sb)],
                                  axis=1).astype(BF16)
            cb = _dot_nt(c_g, b_g.astype(BF16))
            st = state[g]
            y_off = _dot(c_g, st.astype(BF16)) * sc_off[:, g * gw:(g + 1) * gw]
            s_new = _dot(b_g.T.astype(BF16), xend[:, g * gw:(g + 1) * gw])
            state[g] = st * sc_off[SSM_CHUNK - 1:SSM_CHUNK, g * gw:(g + 1) * gw] + s_new
            lane = lax.broadcasted_iota(jnp.int32, (SSM_CHUNK, LANES), 1)
            low = lane < SSM_HEAD_DIM
            for pr in range(gw // LANES):
                col = g * gw + pr * LANES
                h0 = col // SSM_HEAD_DIM
                ws = []
                for hh in (h0, h0 + 1):
                    diff = acs[:, hh:hh + 1] - acs_t[hh:hh + 1, :]
                    ws.append((cb * jnp.exp(jnp.where(tri, diff, -jnp.inf))).astype(BF16))
                xcol = xdt[:, col:col + LANES]
                zero = jnp.zeros_like(xcol)
                rhs = jnp.concatenate([jnp.where(low, xcol, zero), jnp.where(low, zero, xcol)], axis=0)
                y_cols.append(_dot(jnp.concatenate(ws, axis=1), rhs)
                              + y_off[:, pr * LANES:(pr + 1) * LANES])
        ys.append(jnp.concatenate(y_cols, axis=1) + dskip_ref[...] * xs_c)
    y = jnp.concatenate(ys, axis=0) * _silu(z)
    yn = []
    for g in range(SSM_GROUPS):
        yg = y[:, g * gw:(g + 1) * gw]
        yn.append(yg * lax.rsqrt(jnp.mean(yg * yg, axis=-1, keepdims=True) + EPS))
    y_out = (jnp.concatenate(yn, axis=1) * snorm_ref[...]).astype(BF16)

    o_ref[...] = h + _dot(u_out, wout_ref[0:conv_ch, :]) + _dot(y_out, wout_ref[conv_ch:, :])


def _hybrid_mixer(h, bsz, norm_g, w_in, conv_w, conv_b, ln_g, ln_b, sc_w, sc_b, dt_bias, a_log, d_skip,
                  ssm_norm, w_out):
    m, d = h.shape
    ts = HYB_TS
    nt = m // bsz // ts
    conv_ch = conv_w.shape[1]
    xbc_w = sc_w.shape[1]
    inner = SSM_HEADS * SSM_HEAD_DIM
    n_main = w_in.shape[1] - SSM_HEADS
    pad_h = lambda v: jnp.pad(v.reshape(1, SSM_HEADS), ((0, 0), (0, LANES - SSM_HEADS)))
    w_dt = jnp.pad(w_in[:, n_main:], ((0, 0), (0, LANES - SSM_HEADS))).astype(BF16)
    taps_by_block = lambda w, rows: jnp.pad(w, ((0, rows - w.shape[0]), (0, 0))).reshape(
        rows, w.shape[1] // LANES, LANES).transpose(1, 0, 2)
    consts = [norm_g.reshape(1, d), w_in[:, :n_main].astype(BF16), w_dt, taps_by_block(conv_w, CONV_HALO),
              conv_b.reshape(conv_ch // LANES, 1, LANES), ln_g.reshape(1, conv_ch), ln_b.reshape(1, conv_ch),
              taps_by_block(sc_w, SSM_HALO), sc_b.reshape(xbc_w // LANES, 1, LANES), pad_h(dt_bias), pad_h(a_log),
              jnp.repeat(d_skip, SSM_HEAD_DIM).reshape(1, inner), ssm_norm.reshape(1, inner),
              w_out.astype(BF16)]
    row = lambda b, s: (b * nt + s, 0)
    return pl.pallas_call(
        functools.partial(_hyb_kernel, ts=ts),
        grid=(bsz, nt),
        in_specs=[pl.BlockSpec((ts, d), row)] + [_const_spec(c.shape) for c in consts],
        out_specs=pl.BlockSpec((ts, d), row),
        out_shape=jax.ShapeDtypeStruct((m, d), F32),
        scratch_shapes=[
            pltpu.VMEM((conv_ch // LANES, CONV_HALO + ts, LANES), F32),
            pltpu.VMEM((conv_ch // LANES, ts, LANES), F32),
            pltpu.VMEM((xbc_w // LANES, SSM_HALO + ts, LANES), F32),
            pltpu.VMEM((xbc_w // LANES, ts, LANES), F32),
            pltpu.VMEM((SSM_GROUPS, SSM_STATE, inner // SSM_GROUPS), F32),
        ],
        compiler_params=pltpu.CompilerParams(dimension_semantics=("arbitrary", "arbitrary"),
                                             vmem_limit_bytes=VMEM_LIMIT_BYTES),
        name="hybrid_mixer",
    )(h, *consts)


def _attn_kernel(sinks_ref, h_ref, g_ref, wqkv_ref, bqkv_ref, cos_ref, sin_ref, rk_ref, rv_ref, wo_ref, bo_ref,
                 o_ref, kbuf, vbuf, att, *, ts):
    half = ATT_HEAD_DIM // 2
    grp = ATT_HEADS // ATT_KV_HEADS
    gq = grp * ATT_HEAD_DIM
    nq = ATT_HEADS * ATT_HEAD_DIM
    nkv = ATT_KV_HEADS * ATT_HEAD_DIM
    nblk = ts // WINDOW
    first_tile = pl.program_id(1) == 0

    @pl.when(first_tile)
    def _():
        kbuf[:, 0:WINDOW, :] = jnp.zeros((ATT_KV_HEADS, WINDOW, gq), BF16)
        vbuf[:, 0:WINDOW, :] = jnp.zeros((ATT_KV_HEADS, WINDOW, gq), BF16)

    h = h_ref[...]
    hn = _rms(h, g_ref[...]).astype(BF16)
    qkv = _dot(hn, wqkv_ref[...]) + bqkv_ref[...]
    cos = cos_ref[...]
    sin = sin_ref[...]

    def rope(x, y):
        return jnp.concatenate([x * cos - y * sin, y * cos + x * sin], axis=1)

    scale = ATT_HEAD_DIM ** -0.5
    q = [(rope(qkv[:, g * gq:g * gq + LANES], qkv[:, g * gq + LANES:(g + 1) * gq]) * scale).astype(BF16)
         for g in range(ATT_KV_HEADS)]
    k = rope(qkv[:, nq:nq + LANES], qkv[:, nq + LANES:nq + nkv]).astype(BF16)
    v = qkv[:, nq + nkv:].astype(BF16)
    krep = _dot(k, rk_ref[...]).astype(BF16)
    vrep = _dot(v, rv_ref[...]).astype(BF16)
    for g in range(ATT_KV_HEADS):
        kbuf[g, WINDOW:WINDOW + ts, :] = krep[:, g * gq:(g + 1) * gq]
        vbuf[g, WINDOW:WINDOW + ts, :] = vrep[:, g * gq:(g + 1) * gq]

    lane = lax.broadcasted_iota(jnp.int32, (WINDOW, gq), 1)
    lane_in = jnp.bitwise_and(lane, LANES - 1)
    q_slot = [jnp.where(jnp.logical_and(lane_in >= a * half, lane_in < (a + 1) * half), 1.0, 0.0).astype(BF16)
              for a in range(grp)]
    o_from = [lane >= a * ATT_HEAD_DIM for a in range(grp)]
    qi = jnp.bitwise_and(lax.broadcasted_iota(jnp.int32, (grp * WINDOW, WINDOW), 0), WINDOW - 1)
    kj = lax.broadcasted_iota(jnp.int32, (grp * WINDOW, WINDOW), 1)
    in_cur = kj <= qi
    has_prev = kj >= WINDOW * first_tile.astype(jnp.int32)
    srow = lax.broadcasted_iota(jnp.int32, (grp * WINDOW, 1), 0)

    for i in range(nblk):
        rows = slice(i * WINDOW, (i + 1) * WINDOW)
        keys = slice(i * WINDOW, (i + 2) * WINDOW)
        for g in range(ATT_KV_HEADS):
            qg = q[g][rows]
            q4 = jnp.concatenate([qg * q_slot[a] for a in range(grp)], axis=0)
            logits = _dot_nt(q4, kbuf[g, keys, :])
            prev = logits[:, 0:WINDOW]
            if i == 0:
                prev = jnp.where(has_prev, prev, -jnp.inf)
            lg = jnp.where(in_cur, logits[:, WINDOW:], prev)
            sink = sinks_ref[grp * g]
            for a in range(1, grp):
                sink = jnp.where(srow >= a * WINDOW, sinks_ref[grp * g + a], sink)
            mx = jnp.maximum(jnp.max(lg, axis=-1, keepdims=True), sink)
            e = jnp.exp(lg - mx)
            den = jnp.sum(e, axis=-1, keepdims=True) + jnp.exp(sink - mx)
            zero = jnp.zeros_like(e)
            p = jnp.concatenate([jnp.where(in_cur, zero, e), jnp.where(in_cur, e, zero)], axis=1).astype(BF16)
            o4 = _dot(p, vbuf[g, keys, :]) / den
            out = o4[0:WINDOW]
            for a in range(1, grp):
                out = jnp.where(o_from[a], o4[a * WINDOW:(a + 1) * WINDOW], out)
            att[rows, g * gq:(g + 1) * gq] = out.astype(BF16)

    for g in range(ATT_KV_HEADS):
        kbuf[g, 0:WINDOW, :] = krep[ts - WINDOW:ts, g * gq:(g + 1) * gq]
        vbuf[g, 0:WINDOW, :] = vrep[ts - WINDOW:ts, g * gq:(g + 1) * gq]
    o_ref[...] = h + _dot(att[...], wo_ref[...]) + bo_ref[...]


def _attention_mixer(h, bsz, norm_g, w_qkv, b_qkv, sinks, w_o, b_o):
    m, d = h.shape
    ts = ATT_TS
    seqlen = m // bsz
    nt = seqlen // ts
    half = ATT_HEAD_DIM // 2
    grp = ATT_HEADS // ATT_KV_HEADS
    gq = grp * ATT_HEAD_DIM
    nq, nkv = ATT_HEADS * ATT_HEAD_DIM, ATT_KV_HEADS * ATT_HEAD_DIM
    q_cols = [(grp * g + a) * ATT_HEAD_DIM + hf * half + dd
              for g in range(ATT_KV_HEADS) for hf in range(2) for a in range(grp) for dd in range(half)]
    k_cols = [nq + kh * ATT_HEAD_DIM + hf * half + dd
              for hf in range(2) for kh in range(ATT_KV_HEADS) for dd in range(half)]
    cols = jnp.asarray(q_cols + k_cols + list(range(nq + nkv, nq + 2 * nkv)), jnp.int32)
    src = jnp.arange(nkv)[:, None]
    dst = jnp.arange(ATT_KV_HEADS * gq)[None, :]
    g_of, in_g = dst // gq, dst % gq
    rk = (src == (in_g // LANES) * LANES + g_of * half + in_g % half).astype(BF16)
    rv = (src == g_of * ATT_HEAD_DIM + in_g % ATT_HEAD_DIM).astype(BF16)
    inv = ROPE_THETA ** (-jnp.arange(0, ATT_HEAD_DIM, 2, dtype=F32) / ATT_HEAD_DIM)
    ang = jnp.arange(seqlen, dtype=F32)[:, None] * inv[None, :]
    cos, sin = jnp.tile(jnp.cos(ang), (1, LANES // half)), jnp.tile(jnp.sin(ang), (1, LANES // half))
    row = lambda b, s: (b * nt + s, 0)
    tab = lambda b, s: (s, 0)
    consts = [norm_g.reshape(1, d), jnp.take(w_qkv, cols, axis=1).astype(BF16),
              jnp.take(b_qkv, cols).reshape(1, -1)]
    tail = [rk, rv, w_o.astype(BF16), b_o.reshape(1, d)]
    return pl.pallas_call(
        functools.partial(_attn_kernel, ts=ts),
        grid=(bsz, nt),
        in_specs=[pl.BlockSpec(memory_space=pltpu.SMEM), pl.BlockSpec((ts, d), row)]
                 + [_const_spec(c.shape) for c in consts]
                 + [pl.BlockSpec((ts, LANES), tab), pl.BlockSpec((ts, LANES), tab)]
                 + [_const_spec(c.shape) for c in tail],
        out_specs=pl.BlockSpec((ts, d), row),
        out_shape=jax.ShapeDtypeStruct((m, d), F32),
        scratch_shapes=[
            pltpu.VMEM((ATT_KV_HEADS, WINDOW + ts, gq), BF16),
            pltpu.VMEM((ATT_KV_HEADS, WINDOW + ts, gq), BF16),
            pltpu.VMEM((ts, nq), BF16),
        ],
        compiler_params=pltpu.CompilerParams(dimension_semantics=("arbitrary", "arbitrary"),
                                             vmem_limit_bytes=VMEM_LIMIT_BYTES),
        name="attention_mixer",
    )(sinks, h, *consts, cos, sin, *tail)


def kernel(x, p, norm_ffn1, ffn1_w_in, ffn1_w_out, norm_mix, norm_ffn2, ffn2_w_in, ffn2_w_out, ple_norm, ple_gate_w, ple_proj_w, hyb_w_in, conv_dw_w, conv_dw_b, conv_ln_g, conv_ln_b, ssm_conv_w, ssm_conv_b, ssm_dt_bias, ssm_a_log, ssm_d, ssm_norm, hyb_w_out, att_w_qkv, att_b_qkv, att_sinks, att_w_o, att_b_o, final_norm):
    bsz, seqlen, d = x.shape
    depth = p.shape[0]
    m = bsz * seqlen
    h = x.reshape(m, d)
    bf = lambda w: w.astype(BF16)

    for i in range(depth):
        j = i // 2
        h = _ffn(h, norm_ffn1[i], bf(ffn1_w_in[i]), bf(ffn1_w_out[i]), name=f"ffn1_l{i}")
        if i % 2 == 0:
            h = _hybrid_mixer(h, bsz, norm_mix[i], hyb_w_in[j], conv_dw_w[j], conv_dw_b[j], conv_ln_g[j],
                              conv_ln_b[j], ssm_conv_w[j], ssm_conv_b[j], ssm_dt_bias[j], ssm_a_log[j],
                              ssm_d[j], ssm_norm[j], hyb_w_out[j])
        else:
            h = _attention_mixer(h, bsz, norm_mix[i], att_w_qkv[j], att_b_qkv[j], att_sinks[j],
                                 att_w_o[j], att_b_o[j])
        h = _ffn(h, norm_ffn2[i], bf(ffn2_w_in[i]), bf(ffn2_w_out[i]),
                 ple=(p[i].reshape(m, -1), ple_norm[i], bf(ple_gate_w[i]), bf(ple_proj_w[i])),
                 final_g=final_norm if i == depth - 1 else None, name=f"ffn2_l{i}")
    return h.reshape(bsz, seqlen, d)
```

```python
import functools

import jax
import jax.numpy as jnp
from jax import lax
from jax.experimental import pallas as pl
from jax.experimental.pallas import tpu as pltpu

F32 = jnp.float32
BF16 = jnp.bfloat16

EPS = 1e-6
LANES = 128
SUBLANES = 8
VMEM_LIMIT_BYTES = 56 * 1024 * 1024

CONV_WIDTH = 31
CONV_HALO = 32
SSM_HEADS = 16
SSM_HEAD_DIM = 64
SSM_GROUPS = 2
SSM_STATE = 128
SSM_CONV = 4
SSM_HALO = 8
SSM_CHUNK = 128
ATT_HEADS = 16
ATT_KV_HEADS = 4
ATT_HEAD_DIM = 64
WINDOW = 128
ROPE_THETA = 10000.0

FFN_TM = 512
FFN_FC = 256
HYB_TS = 512
CONV_ROWS = 256
PROJ_PIECE = 256
ATT_TS = 512


def _rms(x, g):
    return x * lax.rsqrt(jnp.mean(x * x, axis=-1, keepdims=True) + EPS) * g


def _silu(x):
    return x * jax.nn.sigmoid(x)


def _dot(a, b):
    return jnp.dot(a, b, preferred_element_type=F32)


def _dot_nt(a, b):
    return lax.dot_general(a, b, (((1,), (1,)), ((), ())), preferred_element_type=F32)


def _const_spec(shape):
    nd = len(shape)
    return pl.BlockSpec(shape, lambda *_: (0,) * nd, pipeline_mode=pl.Buffered(1))


def _split_bf16(x, parts):
    out = []
    r = x
    for _ in range(parts):
        p = r.astype(BF16)
        out.append(p)
        r = r - p.astype(F32)
    return out


def _ffn_kernel(*refs, d_ff, fc, with_ple, with_final):
    h_ref, g_ref, win_ref, wout_ref = refs[:4]
    rest = list(refs[4:])
    if with_ple:
        p_ref, pn_ref, pgw_ref, ppw_ref = rest[:4]
        rest = rest[4:]
    if with_final:
        fn_ref = rest[0]
        rest = rest[1:]
    (o_ref,) = rest

    h = h_ref[...]
    xn = _rms(h, g_ref[...]).astype(BF16)
    acc = jnp.zeros(h.shape, F32)
    for c in range(d_ff // fc):
        gate = _dot(xn, win_ref[:, c * fc:(c + 1) * fc])
        up = _dot(xn, win_ref[:, d_ff + c * fc:d_ff + (c + 1) * fc])
        act = (_silu(gate) * up).astype(BF16)
        acc = acc + _dot(act, wout_ref[c * fc:(c + 1) * fc, :])
    h = h + 0.5 * acc
    if with_ple:
        gate = jax.nn.sigmoid(_dot(_rms(h, pn_ref[...]).astype(BF16), pgw_ref[...]))
        h = h + gate * _dot(p_ref[...].astype(BF16), ppw_ref[...])
    if with_final:
        h = _rms(h, fn_ref[...])
    o_ref[...] = h


def _layer_spec(stacked, layer):
    nd = stacked.ndim - 1
    return pl.BlockSpec((None,) + stacked.shape[1:], lambda *_: (layer,) + (0,) * nd,
                        pipeline_mode=pl.Buffered(1))


def _ffn(h, g, w_in, w_out, layer, ple=None, final_g=None, *, name):
    m, d = h.shape
    d_ff = w_out.shape[1]
    tm = FFN_TM
    row = lambda i: (i, 0)
    in_specs = [pl.BlockSpec((tm, d), row), _const_spec((1, d)),
                _layer_spec(w_in, layer), _layer_spec(w_out, layer)]
    args = [h, g.reshape(1, d), w_in, w_out]
    if ple is not None:
        p, pn, pgw, ppw = ple
        in_specs += [pl.BlockSpec((None, tm, p.shape[2]), lambda i: (layer, i, 0)), _const_spec((1, d)),
                     _layer_spec(pgw, layer), _layer_spec(ppw, layer)]
        args += [p, pn.reshape(1, d), pgw, ppw]
    if final_g is not None:
        in_specs.append(_const_spec((1, d)))
        args.append(final_g.reshape(1, d))
    return pl.pallas_call(
        functools.partial(_ffn_kernel, d_ff=d_ff, fc=FFN_FC, with_ple=ple is not None,
                          with_final=final_g is not None),
        grid=(m // tm,),
        in_specs=in_specs,
        out_specs=pl.BlockSpec((tm, d), row),
        out_shape=jax.ShapeDtypeStruct((m, d), F32),
        compiler_params=pltpu.CompilerParams(dimension_semantics=("arbitrary",),
                                             vmem_limit_bytes=VMEM_LIMIT_BYTES),
        name=name,
    )(*args)


def _causal_conv_blocks(buf, out, w_ref, b_ref, *, taps, halo, ts, feeders, act=None):
    n_blocks = buf.shape[0]
    per_feeder = n_blocks // len(feeders)
    for j in range(n_blocks):
        if j % per_feeder == 0:
            feeders[j // per_feeder]()
        w = w_ref[j]
        for r0 in range(0, ts, CONV_ROWS):
            acc = jnp.broadcast_to(b_ref[j], (CONV_ROWS, LANES))
            for k in range(taps):
                off = halo - (taps - 1) + k + r0
                acc = acc + buf[j, off:off + CONV_ROWS, :] * w[k:k + 1, :]
            out[j, r0:r0 + CONV_ROWS, :] = acc if act is None else act(acc)
        buf[j, 0:halo, :] = buf[j, ts:ts + halo, :]


def _hyb_kernel(h_ref, g_ref, win_ref, wdt_ref, cw_ref, cb_ref, lng_ref, lnb_ref, scw_ref, scb_ref,
                dtb_ref, alog_ref, dskip_ref, snorm_ref, wout_ref, o_ref,
                ubuf, ycv, xbuf, xcv, state, *, ts):
    n_cb = ycv.shape[0]
    n_xb = xcv.shape[0]
    conv_ch = n_cb * LANES
    inner = SSM_HEADS * SSM_HEAD_DIM
    gw = inner // SSM_GROUPS
    n_ib = inner // LANES
    n_sb = SSM_STATE // LANES
    pw = PROJ_PIECE
    bpp = pw // LANES

    @pl.when(pl.program_id(1) == 0)
    def _():
        ubuf[:, 0:CONV_HALO, :] = jnp.zeros((n_cb, CONV_HALO, LANES), F32)
        xbuf[:, 0:SSM_HALO, :] = jnp.zeros((n_xb, SSM_HALO, LANES), F32)
        state[...] = jnp.zeros(state.shape, F32)

    h = h_ref[...]
    hn = _rms(h, g_ref[...]).astype(BF16)

    def glu(c):
        val = _dot(hn, win_ref[:, c * pw:(c + 1) * pw])
        gate = _dot(hn, win_ref[:, conv_ch + c * pw:conv_ch + (c + 1) * pw])
        u = val * jax.nn.sigmoid(gate)
        for jj in range(bpp):
            ubuf[c * bpp + jj, CONV_HALO:CONV_HALO + ts, :] = u[:, jj * LANES:(jj + 1) * LANES]

    def xbc_cols(c):
        base = 2 * conv_ch + inner
        x = _dot(hn, win_ref[:, base + c * pw:base + (c + 1) * pw])
        for jj in range(bpp):
            xbuf[c * bpp + jj, SSM_HALO:SSM_HALO + ts, :] = x[:, jj * LANES:(jj + 1) * LANES]

    _causal_conv_blocks(ubuf, ycv, cw_ref, cb_ref, taps=CONV_WIDTH, halo=CONV_HALO, ts=ts,
                        feeders=[functools.partial(glu, c) for c in range(n_cb // bpp)])
    yc = jnp.concatenate([ycv[j] for j in range(n_cb)], axis=1)
    mu = jnp.mean(yc, axis=-1, keepdims=True)
    ycc = yc - mu
    var = jnp.mean(ycc * ycc, axis=-1, keepdims=True)
    u_out = _silu(ycc * lax.rsqrt(var + EPS) * lng_ref[...] + lnb_ref[...]).astype(BF16)

    _causal_conv_blocks(xbuf, xcv, scw_ref, scb_ref, taps=SSM_CONV, halo=SSM_HALO, ts=ts, act=_silu,
                        feeders=[functools.partial(xbc_cols, c) for c in range(n_xb // bpp)])
    z = _dot(hn, win_ref[:, 2 * conv_ch:2 * conv_ch + inner])
    dt_raw = _dot(hn, wdt_ref[...]) + dtb_ref[...]
    dt = jnp.maximum(dt_raw, 0.0) + jnp.log1p(jnp.exp(-jnp.abs(dt_raw)))
    adt = dt * (-jnp.exp(alog_ref[...]))

    ri = lax.broadcasted_iota(jnp.int32, (SSM_CHUNK, SSM_CHUNK), 0)
    ci = lax.broadcasted_iota(jnp.int32, (SSM_CHUNK, SSM_CHUNK), 1)
    tri = ri >= ci
    tri_b = jnp.where(tri, 1.0, 0.0).astype(BF16)
    eh = lax.broadcasted_iota(jnp.int32, (LANES, inner), 0)
    el = lax.broadcasted_iota(jnp.int32, (LANES, inner), 1)
    in_head = jnp.logical_and(el >= eh * SSM_HEAD_DIM, el < (eh + 1) * SSM_HEAD_DIM)
    expand = jnp.where(in_head, 1.0, 0.0).astype(BF16)

    ys = []
    for c in range(ts // SSM_CHUNK):
        rows = slice(c * SSM_CHUNK, (c + 1) * SSM_CHUNK)
        dt_c = dt[rows]
        acs = sum(_dot(tri_b, p) for p in _split_bf16(adt[rows], 3))
        acs_t = acs.T
        last = acs[SSM_CHUNK - 1:SSM_CHUNK, :]
        stack = jnp.concatenate([dt_c, dt_c * jnp.exp(last - acs)], axis=0).astype(BF16)
        wide = _dot(stack, expand)
        dt_x = wide[0:SSM_CHUNK]
        w_end = wide[SSM_CHUNK:]
        sc_off = sum(_dot(p, expand) for p in _split_bf16(jnp.exp(acs), 2))
        xs_c = jnp.concatenate([xcv[j, rows, :] for j in range(n_ib)], axis=1)
        xdt = (xs_c * dt_x).astype(BF16)
        xend = (xs_c * w_end).astype(BF16)
        y_cols = []
        for g in range(SSM_GROUPS):
            b_g = jnp.concatenate([xcv[n_ib + g * n_sb + j, rows, :] for j in range(n_sb)], axis=1)
            c_g = jnp.concatenate([xcv[n_ib + (SSM_GROUPS + g) * n_sb + j, rows, :] for j in range(n_sb)],
                                  axis=1).astype(BF16)
            cb = _dot_nt(c_g, b_g.astype(BF16))
            st = state[g]
            y_off = _dot(c_g, st.astype(BF16)) * sc_off[:, g * gw:(g + 1) * gw]
            s_new = _dot(b_g.T.astype(BF16), xend[:, g * gw:(g + 1) * gw])
            state[g] = st * sc_off[SSM_CHUNK - 1:SSM_CHUNK, g * gw:(g + 1) * gw] + s_new
            lane = lax.broadcasted_iota(jnp.int32, (SSM_CHUNK, LANES), 1)
            low = lane < SSM_HEAD_DIM
            for pr in range(gw // LANES):
                col = g * gw + pr * LANES
                h0 = col // SSM_HEAD_DIM
                ws = []
                for hh in (h0, h0 + 1):
                    diff = acs[:, hh:hh + 1] - acs_t[hh:hh + 1, :]
                    ws.append((cb * jnp.exp(jnp.where(tri, diff, -jnp.inf))).astype(BF16))
                xcol = xdt[:, col:col + LANES]
                zero = jnp.zeros_like(xcol)
                rhs = jnp.concatenate([jnp.where(low, xcol, zero), jnp.where(low, zero, xcol)], axis=0)
                y_cols.append(_dot(jnp.concatenate(ws, axis=1), rhs)
                              + y_off[:, pr * LANES:(pr + 1) * LANES])
        ys.append(jnp.concatenate(y_cols, axis=1) + dskip_ref[...] * xs_c)
    y = jnp.concatenate(ys, axis=0) * _silu(z)
    yn = []
    for g in range(SSM_GROUPS):
        yg = y[:, g * gw:(g + 1) * gw]
        yn.append(yg * lax.rsqrt(jnp.mean(yg * yg, axis=-1, keepdims=True) + EPS))
    y_out = (jnp.concatenate(yn, axis=1) * snorm_ref[...]).astype(BF16)

    o_ref[...] = h + _dot(u_out, wout_ref[0:conv_ch, :]) + _dot(y_out, wout_ref[conv_ch:, :])


def _hybrid_mixer(h, bsz, norm_g, w_in, conv_w, conv_b, ln_g, ln_b, sc_w, sc_b, dt_bias, a_log, d_skip,
                  ssm_norm, w_out):
    m, d = h.shape
    ts = HYB_TS
    nt = m // bsz // ts
    conv_ch = conv_w.shape[1]
    xbc_w = sc_w.shape[1]
    inner = SSM_HEADS * SSM_HEAD_DIM
    n_main = w_in.shape[1] - SSM_HEADS
    pad_h = lambda v: jnp.pad(v.reshape(1, SSM_HEADS), ((0, 0), (0, LANES - SSM_HEADS)))
    w_dt = jnp.pad(w_in[:, n_main:], ((0, 0), (0, LANES - SSM_HEADS))).astype(BF16)
    taps_by_block = lambda w, rows: jnp.pad(w, ((0, rows - w.shape[0]), (0, 0))).reshape(
        rows, w.shape[1] // LANES, LANES).transpose(1, 0, 2)
    consts = [norm_g.reshape(1, d), w_in.astype(BF16), w_dt, taps_by_block(conv_w, CONV_HALO),
              conv_b.reshape(conv_ch // LANES, 1, LANES), ln_g.reshape(1, conv_ch), ln_b.reshape(1, conv_ch),
              taps_by_block(sc_w, SSM_HALO), sc_b.reshape(xbc_w // LANES, 1, LANES), pad_h(dt_bias), pad_h(a_log),
              jnp.repeat(d_skip, SSM_HEAD_DIM).reshape(1, inner), ssm_norm.reshape(1, inner),
              w_out.astype(BF16)]
    row = lambda b, s: (b * nt + s, 0)
    return pl.pallas_call(
        functools.partial(_hyb_kernel, ts=ts),
        grid=(bsz, nt),
        in_specs=[pl.BlockSpec((ts, d), row)] + [_const_spec(c.shape) for c in consts],
        out_specs=pl.BlockSpec((ts, d), row),
        out_shape=jax.ShapeDtypeStruct((m, d), F32),
        scratch_shapes=[
            pltpu.VMEM((conv_ch // LANES, CONV_HALO + ts, LANES), F32),
            pltpu.VMEM((conv_ch // LANES, ts, LANES), F32),
            pltpu.VMEM((xbc_w // LANES, SSM_HALO + ts, LANES), F32),
            pltpu.VMEM((xbc_w // LANES, ts, LANES), F32),
            pltpu.VMEM((SSM_GROUPS, SSM_STATE, inner // SSM_GROUPS), F32),
        ],
        compiler_params=pltpu.CompilerParams(dimension_semantics=("arbitrary", "arbitrary"),
                                             vmem_limit_bytes=VMEM_LIMIT_BYTES),
        name="hybrid_mixer",
    )(h, *consts)


def _attn_kernel(sinks_ref, h_ref, g_ref, wqkv_ref, bqkv_ref, cos_ref, sin_ref, rk_ref, rv_ref, wo_ref, bo_ref,
                 o_ref, kbuf, vbuf, att, *, ts):
    half = ATT_HEAD_DIM // 2
    grp = ATT_HEADS // ATT_KV_HEADS
    gq = grp * ATT_HEAD_DIM
    nq = ATT_HEADS * ATT_HEAD_DIM
    nkv = ATT_KV_HEADS * ATT_HEAD_DIM
    nblk = ts // WINDOW
    first_tile = pl.program_id(1) == 0

    @pl.when(first_tile)
    def _():
        kbuf[:, 0:WINDOW, :] = jnp.zeros((ATT_KV_HEADS, WINDOW, gq), BF16)
        vbuf[:, 0:WINDOW, :] = jnp.zeros((ATT_KV_HEADS, WINDOW, gq), BF16)

    h = h_ref[...]
    hn = _rms(h, g_ref[...]).astype(BF16)
    qkv = _dot(hn, wqkv_ref[...]) + bqkv_ref[...]
    cos = cos_ref[...]
    sin = sin_ref[...]

    def rope(x, y):
        return jnp.concatenate([x * cos - y * sin, y * cos + x * sin], axis=1)

    scale = ATT_HEAD_DIM ** -0.5
    q = [(rope(qkv[:, g * gq:g * gq + LANES], qkv[:, g * gq + LANES:(g + 1) * gq]) * scale).astype(BF16)
         for g in range(ATT_KV_HEADS)]
    k = rope(qkv[:, nq:nq + LANES], qkv[:, nq + LANES:nq + nkv]).astype(BF16)
    v = qkv[:, nq + nkv:].astype(BF16)
    krep = _dot(k, rk_ref[...]).astype(BF16)
    vrep = _dot(v, rv_ref[...]).astype(BF16)
    for g in range(ATT_KV_HEADS):
        kbuf[g, WINDOW:WINDOW + ts, :] = krep[:, g * gq:(g + 1) * gq]
        vbuf[g, WINDOW:WINDOW + ts, :] = vrep[:, g * gq:(g + 1) * gq]

    lane = lax.broadcasted_iota(jnp.int32, (WINDOW, gq), 1)
    lane_in = jnp.bitwise_and(lane, LANES - 1)
    q_slot = [jnp.where(jnp.logical_and(lane_in >= a * half, lane_in < (a + 1) * half), 1.0, 0.0).astype(BF16)
              for a in range(grp)]
    o_from = [lane >= a * ATT_HEAD_DIM for a in range(grp)]
    qi = jnp.bitwise_and(lax.broadcasted_iota(jnp.int32, (grp * WINDOW, WINDOW), 0), WINDOW - 1)
    kj = lax.broadcasted_iota(jnp.int32, (grp * WINDOW, WINDOW), 1)
    in_cur = kj <= qi
    has_prev = kj >= WINDOW * first_tile.astype(jnp.int32)
    srow = lax.broadcasted_iota(jnp.int32, (grp * WINDOW, 1), 0)

    for i in range(nblk):
        rows = slice(i * WINDOW, (i + 1) * WINDOW)
        keys = slice(i * WINDOW, (i + 2) * WINDOW)
        for g in range(ATT_KV_HEADS):
            qg = q[g][rows]
            q4 = jnp.concatenate([qg * q_slot[a] for a in range(grp)], axis=0)
            logits = _dot_nt(q4, kbuf[g, keys, :])
            prev = logits[:, 0:WINDOW]
            if i == 0:
                prev = jnp.where(has_prev, prev, -jnp.inf)
            lg = jnp.where(in_cur, logits[:, WINDOW:], prev)
            sink = sinks_ref[grp * g]
            for a in range(1, grp):
                sink = jnp.where(srow >= a * WINDOW, sinks_ref[grp * g + a], sink)
            mx = jnp.maximum(jnp.max(lg, axis=-1, keepdims=True), sink)
            e = jnp.exp(lg - mx)
            den = jnp.sum(e, axis=-1, keepdims=True) + jnp.exp(sink - mx)
            zero = jnp.zeros_like(e)
            p = jnp.concatenate([jnp.where(in_cur, zero, e), jnp.where(in_cur, e, zero)], axis=1).astype(BF16)
            o4 = _dot(p, vbuf[g, keys, :]) / den
            out = o4[0:WINDOW]
            for a in range(1, grp):
                out = jnp.where(o_from[a], o4[a * WINDOW:(a + 1) * WINDOW], out)
            att[rows, g * gq:(g + 1) * gq] = out.astype(BF16)

    for g in range(ATT_KV_HEADS):
        kbuf[g, 0:WINDOW, :] = krep[ts - WINDOW:ts, g * gq:(g + 1) * gq]
        vbuf[g, 0:WINDOW, :] = vrep[ts - WINDOW:ts, g * gq:(g + 1) * gq]
    o_ref[...] = h + _dot(att[...], wo_ref[...]) + bo_ref[...]


def _attention_mixer(h, bsz, norm_g, w_qkv, b_qkv, sinks, w_o, b_o):
    m, d = h.shape
    ts = ATT_TS
    seqlen = m // bsz
    nt = seqlen // ts
    half = ATT_HEAD_DIM // 2
    grp = ATT_HEADS // ATT_KV_HEADS
    gq = grp * ATT_HEAD_DIM
    nq, nkv = ATT_HEADS * ATT_HEAD_DIM, ATT_KV_HEADS * ATT_HEAD_DIM
    q_cols = [(grp * g + a) * ATT_HEAD_DIM + hf * half + dd
              for g in range(ATT_KV_HEADS) for hf in range(2) for a in range(grp) for dd in range(half)]
    k_cols = [nq + kh * ATT_HEAD_DIM + hf * half + dd
              for hf in range(2) for kh in range(ATT_KV_HEADS) for dd in range(half)]
    cols = jnp.asarray(q_cols + k_cols + list(range(nq + nkv, nq + 2 * nkv)), jnp.int32)
    src = jnp.arange(nkv)[:, None]
    dst = jnp.arange(ATT_KV_HEADS * gq)[None, :]
    g_of, in_g = dst // gq, dst % gq
    rk = (src == (in_g // LANES) * LANES + g_of * half + in_g % half).astype(BF16)
    rv = (src == g_of * ATT_HEAD_DIM + in_g % ATT_HEAD_DIM).astype(BF16)
    inv = ROPE_THETA ** (-jnp.arange(0, ATT_HEAD_DIM, 2, dtype=F32) / ATT_HEAD_DIM)
    ang = jnp.arange(seqlen, dtype=F32)[:, None] * inv[None, :]
    cos, sin = jnp.tile(jnp.cos(ang), (1, LANES // half)), jnp.tile(jnp.sin(ang), (1, LANES // half))
    row = lambda b, s: (b * nt + s, 0)
    tab = lambda b, s: (s, 0)
    consts = [norm_g.reshape(1, d), jnp.take(w_qkv, cols, axis=1).astype(BF16),
              jnp.take(b_qkv, cols).reshape(1, -1)]
    tail = [rk, rv, w_o.astype(BF16), b_o.reshape(1, d)]
    return pl.pallas_call(
        functools.partial(_attn_kernel, ts=ts),
        grid=(bsz, nt),
        in_specs=[pl.BlockSpec(memory_space=pltpu.SMEM), pl.BlockSpec((ts, d), row)]
                 + [_const_spec(c.shape) for c in consts]
                 + [pl.BlockSpec((ts, LANES), tab), pl.BlockSpec((ts, LANES), tab)]
                 + [_const_spec(c.shape) for c in tail],
        out_specs=pl.BlockSpec((ts, d), row),
        out_shape=jax.ShapeDtypeStruct((m, d), F32),
        scratch_shapes=[
            pltpu.VMEM((ATT_KV_HEADS, WINDOW + ts, gq), BF16),
            pltpu.VMEM((ATT_KV_HEADS, WINDOW + ts, gq), BF16),
            pltpu.VMEM((ts, nq), BF16),
        ],
        compiler_params=pltpu.CompilerParams(dimension_semantics=("arbitrary", "arbitrary"),
                                             vmem_limit_bytes=VMEM_LIMIT_BYTES),
        name="attention_mixer",
    )(sinks, h, *consts, cos, sin, *tail)


def kernel(x, p, norm_ffn1, ffn1_w_in, ffn1_w_out, norm_mix, norm_ffn2, ffn2_w_in, ffn2_w_out, ple_norm, ple_gate_w, ple_proj_w, hyb_w_in, conv_dw_w, conv_dw_b, conv_ln_g, conv_ln_b, ssm_conv_w, ssm_conv_b, ssm_dt_bias, ssm_a_log, ssm_d, ssm_norm, hyb_w_out, att_w_qkv, att_b_qkv, att_sinks, att_w_o, att_b_o, final_norm):
    bsz, seqlen, d = x.shape
    depth = p.shape[0]
    m = bsz * seqlen
    h = x.reshape(m, d)
    bf = lambda w: w.astype(BF16)
    w1_in, w1_out, w2_in, w2_out = bf(ffn1_w_in), bf(ffn1_w_out), bf(ffn2_w_in), bf(ffn2_w_out)
    ple_gw, ple_pw = bf(ple_gate_w), bf(ple_proj_w)
    p_rows = p.reshape(depth, m, -1)

    for i in range(depth):
        j = i // 2
        h = _ffn(h, norm_ffn1[i], w1_in, w1_out, i, name=f"ffn1_l{i}")
        if i % 2 == 0:
            h = _hybrid_mixer(h, bsz, norm_mix[i], hyb_w_in[j], conv_dw_w[j], conv_dw_b[j], conv_ln_g[j],
                              conv_ln_b[j], ssm_conv_w[j], ssm_conv_b[j], ssm_dt_bias[j], ssm_a_log[j],
                              ssm_d[j], ssm_norm[j], hyb_w_out[j])
        else:
            h = _attention_mixer(h, bsz, norm_mix[i], att_w_qkv[j], att_b_qkv[j], att_sinks[j],
                                 att_w_o[j], att_b_o[j])
        h = _ffn(h, norm_ffn2[i], w2_in, w2_out, i, ple=(p_rows, ple_norm[i], ple_gw, ple_pw),
                 final_g=final_norm if i == depth - 1 else None, name=f"ffn2_l{i}")
    return h.reshape(bsz, seqlen, d)
```

```python
import functools

import jax
import jax.numpy as jnp
from jax import lax
from jax.experimental import pallas as pl
from jax.experimental.pallas import tpu as pltpu

F32 = jnp.float32
BF16 = jnp.bfloat16

EPS = 1e-6
LANES = 128
SUBLANES = 8
VMEM_LIMIT_BYTES = 56 * 1024 * 1024

CONV_WIDTH = 31
CONV_HALO = 32
SSM_HEADS = 16
SSM_HEAD_DIM = 64
SSM_GROUPS = 2
SSM_STATE = 128
SSM_CONV = 4
SSM_HALO = 8
SSM_CHUNK = 128
ATT_HEADS = 16
ATT_KV_HEADS = 4
ATT_HEAD_DIM = 64
WINDOW = 128
ROPE_THETA = 10000.0

FFN_TM = 1024
FFN_FC = 256
HYB_TS = 512
CONV_ROWS = 256
PROJ_PIECE = 256
ATT_TS = 512


def _rms(x, g):
    return x * lax.rsqrt(jnp.mean(x * x, axis=-1, keepdims=True) + EPS) * g


def _silu(x):
    return x * jax.nn.sigmoid(x)


def _dot(a, b):
    return jnp.dot(a, b, preferred_element_type=F32)


def _dot_nt(a, b):
    return lax.dot_general(a, b, (((1,), (1,)), ((), ())), preferred_element_type=F32)


def _const_spec(shape):
    nd = len(shape)
    return pl.BlockSpec(shape, lambda *_: (0,) * nd, pipeline_mode=pl.Buffered(1))


def _split_bf16(x, parts):
    out = []
    r = x
    for _ in range(parts):
        p = r.astype(BF16)
        out.append(p)
        r = r - p.astype(F32)
    return out


def _ffn_kernel(*refs, d_ff, fc, with_ple, with_final):
    h_ref, g_ref, win_ref, wout_ref = refs[:4]
    rest = list(refs[4:])
    if with_ple:
        p_ref, pn_ref, pgw_ref, ppw_ref = rest[:4]
        rest = rest[4:]
    if with_final:
        fn_ref = rest[0]
        rest = rest[1:]
    (o_ref,) = rest

    h = h_ref[...]
    xn = _rms(h, g_ref[...]).astype(BF16)
    acc = jnp.zeros(h.shape, F32)
    for c in range(d_ff // fc):
        gate = _dot(xn, win_ref[:, c * fc:(c + 1) * fc])
        up = _dot(xn, win_ref[:, d_ff + c * fc:d_ff + (c + 1) * fc])
        act = (_silu(gate) * up).astype(BF16)
        acc = acc + _dot(act, wout_ref[c * fc:(c + 1) * fc, :])
    h = h + 0.5 * acc
    if with_ple:
        gate = jax.nn.sigmoid(_dot(_rms(h, pn_ref[...]).astype(BF16), pgw_ref[...]))
        h = h + gate * _dot(p_ref[...].astype(BF16), ppw_ref[...])
    if with_final:
        h = _rms(h, fn_ref[...])
    o_ref[...] = h


def _layer_spec(stacked, layer):
    nd = stacked.ndim - 1
    return pl.BlockSpec((None,) + stacked.shape[1:], lambda *_: (layer,) + (0,) * nd,
                        pipeline_mode=pl.Buffered(1))


def _ffn(h, g, w_in, w_out, layer, ple=None, final_g=None, *, name):
    m, d = h.shape
    d_ff = w_out.shape[1]
    tm = FFN_TM
    row = lambda i: (i, 0)
    in_specs = [pl.BlockSpec((tm, d), row), _const_spec((1, d)),
                _layer_spec(w_in, layer), _layer_spec(w_out, layer)]
    args = [h, g.reshape(1, d), w_in, w_out]
    if ple is not None:
        p, pn, pgw, ppw = ple
        in_specs += [pl.BlockSpec((None, tm, p.shape[2]), lambda i: (layer, i, 0)), _const_spec((1, d)),
                     _layer_spec(pgw, layer), _layer_spec(ppw, layer)]
        args += [p, pn.reshape(1, d), pgw, ppw]
    if final_g is not None:
        in_specs.append(_const_spec((1, d)))
        args.append(final_g.reshape(1, d))
    return pl.pallas_call(
        functools.partial(_ffn_kernel, d_ff=d_ff, fc=FFN_FC, with_ple=ple is not None,
                          with_final=final_g is not None),
        grid=(m // tm,),
        in_specs=in_specs,
        out_specs=pl.BlockSpec((tm, d), row),
        out_shape=jax.ShapeDtypeStruct((m, d), F32),
        compiler_params=pltpu.CompilerParams(dimension_semantics=("arbitrary",),
                                             vmem_limit_bytes=VMEM_LIMIT_BYTES),
        name=name,
    )(*args)


def _causal_conv_blocks(buf, out, w_ref, b_ref, *, taps, halo, ts, feeders, act=None):
    n_blocks = buf.shape[0]
    per_feeder = n_blocks // len(feeders)
    for j in range(n_blocks):
        if j % per_feeder == 0:
            feeders[j // per_feeder]()
        w = w_ref[j]
        for r0 in range(0, ts, CONV_ROWS):
            acc = jnp.broadcast_to(b_ref[j], (CONV_ROWS, LANES))
            for k in range(taps):
                off = halo - (taps - 1) + k + r0
                acc = acc + buf[j, off:off + CONV_ROWS, :] * w[k:k + 1, :]
            out[j, r0:r0 + CONV_ROWS, :] = acc if act is None else act(acc)
        buf[j, 0:halo, :] = buf[j, ts:ts + halo, :]


def _hyb_kernel(h_ref, g_ref, win_ref, wdt_ref, cw_ref, cb_ref, lng_ref, lnb_ref, scw_ref, scb_ref,
                dtb_ref, alog_ref, dskip_ref, snorm_ref, wout_ref, o_ref,
                ubuf, ycv, xbuf, xcv, state, *, ts):
    n_cb = ycv.shape[0]
    n_xb = xcv.shape[0]
    conv_ch = n_cb * LANES
    inner = SSM_HEADS * SSM_HEAD_DIM
    gw = inner // SSM_GROUPS
    n_ib = inner // LANES
    n_sb = SSM_STATE // LANES
    pw = PROJ_PIECE
    bpp = pw // LANES

    @pl.when(pl.program_id(1) == 0)
    def _():
        ubuf[:, 0:CONV_HALO, :] = jnp.zeros((n_cb, CONV_HALO, LANES), F32)
        xbuf[:, 0:SSM_HALO, :] = jnp.zeros((n_xb, SSM_HALO, LANES), F32)
        state[...] = jnp.zeros(state.shape, F32)

    h = h_ref[...]
    hn = _rms(h, g_ref[...]).astype(BF16)

    def glu(c):
        val = _dot(hn, win_ref[:, c * pw:(c + 1) * pw])
        gate = _dot(hn, win_ref[:, conv_ch + c * pw:conv_ch + (c + 1) * pw])
        u = val * jax.nn.sigmoid(gate)
        for jj in range(bpp):
            ubuf[c * bpp + jj, CONV_HALO:CONV_HALO + ts, :] = u[:, jj * LANES:(jj + 1) * LANES]

    def xbc_cols(c):
        base = 2 * conv_ch + inner
        x = _dot(hn, win_ref[:, base + c * pw:base + (c + 1) * pw])
        for jj in range(bpp):
            xbuf[c * bpp + jj, SSM_HALO:SSM_HALO + ts, :] = x[:, jj * LANES:(jj + 1) * LANES]

    _causal_conv_blocks(ubuf, ycv, cw_ref, cb_ref, taps=CONV_WIDTH, halo=CONV_HALO, ts=ts,
                        feeders=[functools.partial(glu, c) for c in range(n_cb // bpp)])
    yc = jnp.concatenate([ycv[j] for j in range(n_cb)], axis=1)
    mu = jnp.mean(yc, axis=-1, keepdims=True)
    ycc = yc - mu
    var = jnp.mean(ycc * ycc, axis=-1, keepdims=True)
    u_out = _silu(ycc * lax.rsqrt(var + EPS) * lng_ref[...] + lnb_ref[...]).astype(BF16)

    _causal_conv_blocks(xbuf, xcv, scw_ref, scb_ref, taps=SSM_CONV, halo=SSM_HALO, ts=ts, act=_silu,
                        feeders=[functools.partial(xbc_cols, c) for c in range(n_xb // bpp)])
    z = _dot(hn, win_ref[:, 2 * conv_ch:2 * conv_ch + inner])
    dt_raw = _dot(hn, wdt_ref[...]) + dtb_ref[...]
    dt = jnp.maximum(dt_raw, 0.0) + jnp.log1p(jnp.exp(-jnp.abs(dt_raw)))
    adt = dt * (-jnp.exp(alog_ref[...]))

    ri = lax.broadcasted_iota(jnp.int32, (SSM_CHUNK, SSM_CHUNK), 0)
    ci = lax.broadcasted_iota(jnp.int32, (SSM_CHUNK, SSM_CHUNK), 1)
    tri = ri >= ci
    tri_b = jnp.where(tri, 1.0, 0.0).astype(BF16)
    eh = lax.broadcasted_iota(jnp.int32, (LANES, inner), 0)
    el = lax.broadcasted_iota(jnp.int32, (LANES, inner), 1)
    in_head = jnp.logical_and(el >= eh * SSM_HEAD_DIM, el < (eh + 1) * SSM_HEAD_DIM)
    expand = jnp.where(in_head, 1.0, 0.0).astype(BF16)

    ys = []
    for c in range(ts // SSM_CHUNK):
        rows = slice(c * SSM_CHUNK, (c + 1) * SSM_CHUNK)
        dt_c = dt[rows]
        acs = sum(_dot(tri_b, p) for p in _split_bf16(adt[rows], 3))
        acs_t = acs.T
        last = acs[SSM_CHUNK - 1:SSM_CHUNK, :]
        stack = jnp.concatenate([dt_c, dt_c * jnp.exp(last - acs)], axis=0).astype(BF16)
        wide = _dot(stack, expand)
        dt_x = wide[0:SSM_CHUNK]
        w_end = wide[SSM_CHUNK:]
        sc_off = sum(_dot(p, expand) for p in _split_bf16(jnp.exp(acs), 2))
        xs_c = jnp.concatenate([xcv[j, rows, :] for j in range(n_ib)], axis=1)
        xdt = (xs_c * dt_x).astype(BF16)
        xend = (xs_c * w_end).astype(BF16)
        y_cols = []
        for g in range(SSM_GROUPS):
            b_g = jnp.concatenate([xcv[n_ib + g * n_sb + j, rows, :] for j in range(n_sb)], axis=1)
            c_g = jnp.concatenate([xcv[n_ib + (SSM_GROUPS + g) * n_sb + j, rows, :] for j in range(n_sb)],
                                  axis=1).astype(BF16)
            cb = _dot_nt(c_g, b_g.astype(BF16))
            st = state[g]
            y_off = _dot(c_g, st.astype(BF16)) * sc_off[:, g * gw:(g + 1) * gw]
            s_new = _dot(b_g.T.astype(BF16), xend[:, g * gw:(g + 1) * gw])
            state[g] = st * sc_off[SSM_CHUNK - 1:SSM_CHUNK, g * gw:(g + 1) * gw] + s_new
            lane = lax.broadcasted_iota(jnp.int32, (SSM_CHUNK, LANES), 1)
            low = lane < SSM_HEAD_DIM
            for pr in range(gw // LANES):
                col = g * gw + pr * LANES
                h0 = col // SSM_HEAD_DIM
                ws = []
                for hh in (h0, h0 + 1):
                    diff = acs[:, hh:hh + 1] - acs_t[hh:hh + 1, :]
                    ws.append((cb * jnp.exp(jnp.where(tri, diff, -jnp.inf))).astype(BF16))
                xcol = xdt[:, col:col + LANES]
                zero = jnp.zeros_like(xcol)
                rhs = jnp.concatenate([jnp.where(low, xcol, zero), jnp.where(low, zero, xcol)], axis=0)
                y_cols.append(_dot(jnp.concatenate(ws, axis=1), rhs)
                              + y_off[:, pr * LANES:(pr + 1) * LANES])
        ys.append(jnp.concatenate(y_cols, axis=1) + dskip_ref[...] * xs_c)
    y = jnp.concatenate(ys, axis=0) * _silu(z)
    yn = []
    for g in range(SSM_GROUPS):
        yg = y[:, g * gw:(g + 1) * gw]
        yn.append(yg * lax.rsqrt(jnp.mean(yg * yg, axis=-1, keepdims=True) + EPS))
    y_out = (jnp.concatenate(yn, axis=1) * snorm_ref[...]).astype(BF16)

    o_ref[...] = h + _dot(u_out, wout_ref[0:conv_ch, :]) + _dot(y_out, wout_ref[conv_ch:, :])


def _hybrid_mixer(h, bsz, norm_g, w_in, conv_w, conv_b, ln_g, ln_b, sc_w, sc_b, dt_bias, a_log, d_skip,
                  ssm_norm, w_out):
    m, d = h.shape
    ts = HYB_TS
    nt = m // bsz // ts
    conv_ch = conv_w.shape[1]
    xbc_w = sc_w.shape[1]
    inner = SSM_HEADS * SSM_HEAD_DIM
    n_main = w_in.shape[1] - SSM_HEADS
    pad_h = lambda v: jnp.pad(v.reshape(1, SSM_HEADS), ((0, 0), (0, LANES - SSM_HEADS)))
    w_dt = jnp.pad(w_in[:, n_main:], ((0, 0), (0, LANES - SSM_HEADS))).astype(BF16)
    taps_by_block = lambda w, rows: jnp.pad(w, ((0, rows - w.shape[0]), (0, 0))).reshape(
        rows, w.shape[1] // LANES, LANES).transpose(1, 0, 2)
    consts = [norm_g.reshape(1, d), w_in.astype(BF16), w_dt, taps_by_block(conv_w, CONV_HALO),
              conv_b.reshape(conv_ch // LANES, 1, LANES), ln_g.reshape(1, conv_ch), ln_b.reshape(1, conv_ch),
              taps_by_block(sc_w, SSM_HALO), sc_b.reshape(xbc_w // LANES, 1, LANES), pad_h(dt_bias), pad_h(a_log),
              jnp.repeat(d_skip, SSM_HEAD_DIM).reshape(1, inner), ssm_norm.reshape(1, inner),
              w_out.astype(BF16)]
    row = lambda b, s: (b * nt + s, 0)
    return pl.pallas_call(
        functools.partial(_hyb_kernel, ts=ts),
        grid=(bsz, nt),
        in_specs=[pl.BlockSpec((ts, d), row)] + [_const_spec(c.shape) for c in consts],
        out_specs=pl.BlockSpec((ts, d), row),
        out_shape=jax.ShapeDtypeStruct((m, d), F32),
        scratch_shapes=[
            pltpu.VMEM((conv_ch // LANES, CONV_HALO + ts, LANES), F32),
            pltpu.VMEM((conv_ch // LANES, ts, LANES), F32),
            pltpu.VMEM((xbc_w // LANES, SSM_HALO + ts, LANES), F32),
            pltpu.VMEM((xbc_w // LANES, ts, LANES), F32),
            pltpu.VMEM((SSM_GROUPS, SSM_STATE, inner // SSM_GROUPS), F32),
        ],
        compiler_params=pltpu.CompilerParams(dimension_semantics=("arbitrary", "arbitrary"),
                                             vmem_limit_bytes=VMEM_LIMIT_BYTES),
        name="hybrid_mixer",
    )(h, *consts)


def _attn_kernel(sinks_ref, h_ref, g_ref, wqkv_ref, bqkv_ref, cos_ref, sin_ref, rk_ref, rv_ref, wo_ref, bo_ref,
                 o_ref, kbuf, vbuf, att, *, ts):
    half = ATT_HEAD_DIM // 2
    grp = ATT_HEADS // ATT_KV_HEADS
    gq = grp * ATT_HEAD_DIM
    nq = ATT_HEADS * ATT_HEAD_DIM
    nkv = ATT_KV_HEADS * ATT_HEAD_DIM
    nblk = ts // WINDOW
    first_tile = pl.program_id(1) == 0

    @pl.when(first_tile)
    def _():
        kbuf[:, 0:WINDOW, :] = jnp.zeros((ATT_KV_HEADS, WINDOW, gq), BF16)
        vbuf[:, 0:WINDOW, :] = jnp.zeros((ATT_KV_HEADS, WINDOW, gq), BF16)

    h = h_ref[...]
    hn = _rms(h, g_ref[...]).astype(BF16)
    qkv = _dot(hn, wqkv_ref[...]) + bqkv_ref[...]
    cos = cos_ref[...]
    sin = sin_ref[...]

    def rope(x, y):
        return jnp.concatenate([x * cos - y * sin, y * cos + x * sin], axis=1)

    scale = ATT_HEAD_DIM ** -0.5
    q = [(rope(qkv[:, g * gq:g * gq + LANES], qkv[:, g * gq + LANES:(g + 1) * gq]) * scale).astype(BF16)
         for g in range(ATT_KV_HEADS)]
    k = rope(qkv[:, nq:nq + LANES], qkv[:, nq + LANES:nq + nkv]).astype(BF16)
    v = qkv[:, nq + nkv:].astype(BF16)
    krep = _dot(k, rk_ref[...]).astype(BF16)
    vrep = _dot(v, rv_ref[...]).astype(BF16)
    for g in range(ATT_KV_HEADS):
        kbuf[g, WINDOW:WINDOW + ts, :] = krep[:, g * gq:(g + 1) * gq]
        vbuf[g, WINDOW:WINDOW + ts, :] = vrep[:, g * gq:(g + 1) * gq]

    lane = lax.broadcasted_iota(jnp.int32, (WINDOW, gq), 1)
    lane_in = jnp.bitwise_and(lane, LANES - 1)
    q_slot = [jnp.where(jnp.logical_and(lane_in >= a * half, lane_in < (a + 1) * half), 1.0, 0.0).astype(BF16)
              for a in range(grp)]
    o_from = [lane >= a * ATT_HEAD_DIM for a in range(grp)]
    qi = jnp.bitwise_and(lax.broadcasted_iota(jnp.int32, (grp * WINDOW, WINDOW), 0), WINDOW - 1)
    kj = lax.broadcasted_iota(jnp.int32, (grp * WINDOW, WINDOW), 1)
    in_cur = kj <= qi
    has_prev = kj >= WINDOW * first_tile.astype(jnp.int32)
    srow = lax.broadcasted_iota(jnp.int32, (grp * WINDOW, 1), 0)

    for i in range(nblk):
        rows = slice(i * WINDOW, (i + 1) * WINDOW)
        keys = slice(i * WINDOW, (i + 2) * WINDOW)
        for g in range(ATT_KV_HEADS):
            qg = q[g][rows]
            q4 = jnp.concatenate([qg * q_slot[a] for a in range(grp)], axis=0)
            logits = _dot_nt(q4, kbuf[g, keys, :])
            prev = logits[:, 0:WINDOW]
            if i == 0:
                prev = jnp.where(has_prev, prev, -jnp.inf)
            lg = jnp.where(in_cur, logits[:, WINDOW:], prev)
            sink = sinks_ref[grp * g]
            for a in range(1, grp):
                sink = jnp.where(srow >= a * WINDOW, sinks_ref[grp * g + a], sink)
            mx = jnp.maximum(jnp.max(lg, axis=-1, keepdims=True), sink)
            e = jnp.exp(lg - mx)
            den = jnp.sum(e, axis=-1, keepdims=True) + jnp.exp(sink - mx)
            zero = jnp.zeros_like(e)
            p = jnp.concatenate([jnp.where(in_cur, zero, e), jnp.where(in_cur, e, zero)], axis=1).astype(BF16)
            o4 = _dot(p, vbuf[g, keys, :]) / den
            out = o4[0:WINDOW]
            for a in range(1, grp):
                out = jnp.where(o_from[a], o4[a * WINDOW:(a + 1) * WINDOW], out)
            att[rows, g * gq:(g + 1) * gq] = out.astype(BF16)

    for g in range(ATT_KV_HEADS):
        kbuf[g, 0:WINDOW, :] = krep[ts - WINDOW:ts, g * gq:(g + 1) * gq]
        vbuf[g, 0:WINDOW, :] = vrep[ts - WINDOW:ts, g * gq:(g + 1) * gq]
    o_ref[...] = h + _dot(att[...], wo_ref[...]) + bo_ref[...]


def _attention_mixer(h, bsz, norm_g, w_qkv, b_qkv, sinks, w_o, b_o):
    m, d = h.shape
    ts = ATT_TS
    seqlen = m // bsz
    nt = seqlen // ts
    half = ATT_HEAD_DIM // 2
    grp = ATT_HEADS // ATT_KV_HEADS
    gq = grp * ATT_HEAD_DIM
    nq, nkv = ATT_HEADS * ATT_HEAD_DIM, ATT_KV_HEADS * ATT_HEAD_DIM
    q_cols = [(grp * g + a) * ATT_HEAD_DIM + hf * half + dd
              for g in range(ATT_KV_HEADS) for hf in range(2) for a in range(grp) for dd in range(half)]
    k_cols = [nq + kh * ATT_HEAD_DIM + hf * half + dd
              for hf in range(2) for kh in range(ATT_KV_HEADS) for dd in range(half)]
    cols = jnp.asarray(q_cols + k_cols + list(range(nq + nkv, nq + 2 * nkv)), jnp.int32)
    src = jnp.arange(nkv)[:, None]
    dst = jnp.arange(ATT_KV_HEADS * gq)[None, :]
    g_of, in_g = dst // gq, dst % gq
    rk = (src == (in_g // LANES) * LANES + g_of * half + in_g % half).astype(BF16)
    rv = (src == g_of * ATT_HEAD_DIM + in_g % ATT_HEAD_DIM).astype(BF16)
    inv = ROPE_THETA ** (-jnp.arange(0, ATT_HEAD_DIM, 2, dtype=F32) / ATT_HEAD_DIM)
    ang = jnp.arange(seqlen, dtype=F32)[:, None] * inv[None, :]
    cos, sin = jnp.tile(jnp.cos(ang), (1, LANES // half)), jnp.tile(jnp.sin(ang), (1, LANES // half))
    row = lambda b, s: (b * nt + s, 0)
    tab = lambda b, s: (s, 0)
    consts = [norm_g.reshape(1, d), jnp.take(w_qkv, cols, axis=1).astype(BF16),
              jnp.take(b_qkv, cols).reshape(1, -1)]
    tail = [rk, rv, w_o.astype(BF16), b_o.reshape(1, d)]
    return pl.pallas_call(
        functools.partial(_attn_kernel, ts=ts),
        grid=(bsz, nt),
        in_specs=[pl.BlockSpec(memory_space=pltpu.SMEM), pl.BlockSpec((ts, d), row)]
                 + [_const_spec(c.shape) for c in consts]
                 + [pl.BlockSpec((ts, LANES), tab), pl.BlockSpec((ts, LANES), tab)]
                 + [_const_spec(c.shape) for c in tail],
        out_specs=pl.BlockSpec((ts, d), row),
        out_shape=jax.ShapeDtypeStruct((m, d), F32),
        scratch_shapes=[
            pltpu.VMEM((ATT_KV_HEADS, WINDOW + ts, gq), BF16),
            pltpu.VMEM((ATT_KV_HEADS, WINDOW + ts, gq), BF16),
            pltpu.VMEM((ts, nq), BF16),
        ],
        compiler_params=pltpu.CompilerParams(dimension_semantics=("arbitrary", "arbitrary"),
                                             vmem_limit_bytes=VMEM_LIMIT_BYTES),
        name="attention_mixer",
    )(sinks, h, *consts, cos, sin, *tail)


def kernel(x, p, norm_ffn1, ffn1_w_in, ffn1_w_out, norm_mix, norm_ffn2, ffn2_w_in, ffn2_w_out, ple_norm, ple_gate_w, ple_proj_w, hyb_w_in, conv_dw_w, conv_dw_b, conv_ln_g, conv_ln_b, ssm_conv_w, ssm_conv_b, ssm_dt_bias, ssm_a_log, ssm_d, ssm_norm, hyb_w_out, att_w_qkv, att_b_qkv, att_sinks, att_w_o, att_b_o, final_norm):
    bsz, seqlen, d = x.shape
    depth = p.shape[0]
    m = bsz * seqlen
    h = x.reshape(m, d)
    bf = lambda w: w.astype(BF16)
    w1_in, w1_out, w2_in, w2_out = bf(ffn1_w_in), bf(ffn1_w_out), bf(ffn2_w_in), bf(ffn2_w_out)
    ple_gw, ple_pw = bf(ple_gate_w), bf(ple_proj_w)
    p_rows = p.reshape(depth, m, -1)

    for i in range(depth):
        j = i // 2
        h = _ffn(h, norm_ffn1[i], w1_in, w1_out, i, name=f"ffn1_l{i}")
        if i % 2 == 0:
            h = _hybrid_mixer(h, bsz, norm_mix[i], hyb_w_in[j], conv_dw_w[j], conv_dw_b[j], conv_ln_g[j],
                              conv_ln_b[j], ssm_conv_w[j], ssm_conv_b[j], ssm_dt_bias[j], ssm_a_log[j],
                              ssm_d[j], ssm_norm[j], hyb_w_out[j])
        else:
            h = _attention_mixer(h, bsz, norm_mix[i], att_w_qkv[j], att_b_qkv[j], att_sinks[j],
                                 att_w_o[j], att_b_o[j])
        h = _ffn(h, norm_ffn2[i], w2_in, w2_out, i, ple=(p_rows, ple_norm[i], ple_gw, ple_pw),
                 final_g=final_norm if i == depth - 1 else None, name=f"ffn2_l{i}")
    return h.reshape(bsz, seqlen, d)
```

```python
import functools

import jax
import jax.numpy as jnp
from jax import lax
from jax.experimental import pallas as pl
from jax.experimental.pallas import tpu as pltpu

F32 = jnp.float32
BF16 = jnp.bfloat16

EPS = 1e-6
LANES = 128
SUBLANES = 8
VMEM_LIMIT_BYTES = 56 * 1024 * 1024

CONV_WIDTH = 31
CONV_HALO = 32
SSM_HEADS = 16
SSM_HEAD_DIM = 64
SSM_GROUPS = 2
SSM_STATE = 128
SSM_CONV = 4
SSM_HALO = 8
SSM_CHUNK = 128
ATT_HEADS = 16
ATT_KV_HEADS = 4
ATT_HEAD_DIM = 64
WINDOW = 128
ROPE_THETA = 10000.0

FFN_TM = 1024
CAST_ROWS = 128
BF16_ROWS = 16
FFN_FC = 256
HYB_TS = 512
CONV_ROWS = 256
PROJ_PIECE = 256
ATT_TS = 512


def _rms(x, g):
    return x * lax.rsqrt(jnp.mean(x * x, axis=-1, keepdims=True) + EPS) * g


def _silu(x):
    return x * jax.nn.sigmoid(x)


def _dot(a, b):
    return jnp.dot(a, b, preferred_element_type=F32)


def _dot_nt(a, b):
    return lax.dot_general(a, b, (((1,), (1,)), ((), ())), preferred_element_type=F32)


def _const_spec(shape):
    nd = len(shape)
    return pl.BlockSpec(shape, lambda *_: (0,) * nd, pipeline_mode=pl.Buffered(1))


def _split_bf16(x, parts):
    out = []
    r = x
    for _ in range(parts):
        p = r.astype(BF16)
        out.append(p)
        r = r - p.astype(F32)
    return out


def _ffn_kernel(*refs, d_ff, fc, with_ple, with_final, with_cast):
    h_ref, g_ref, win_ref, wout_ref = refs[:4]
    rest = list(refs[4:])
    if with_ple:
        p_ref, pn_ref, pgw_ref, ppw_ref = rest[:4]
        rest = rest[4:]
    if with_final:
        fn_ref = rest[0]
        rest = rest[1:]
    if with_cast:
        nin_ref, nout_ref, o_ref, nin_o, nout_o = rest
    else:
        (o_ref,) = rest

    h = h_ref[...]
    xn = _rms(h, g_ref[...]).astype(BF16)
    acc = jnp.zeros(h.shape, F32)
    for c in range(d_ff // fc):
        gate = _dot(xn, win_ref[:, c * fc:(c + 1) * fc])
        up = _dot(xn, win_ref[:, d_ff + c * fc:d_ff + (c + 1) * fc])
        act = (_silu(gate) * up).astype(BF16)
        acc = acc + _dot(act, wout_ref[c * fc:(c + 1) * fc, :])
    h = h + 0.5 * acc
    if with_ple:
        gate = jax.nn.sigmoid(_dot(_rms(h, pn_ref[...]).astype(BF16), pgw_ref[...]))
        h = h + gate * _dot(p_ref[...].astype(BF16), ppw_ref[...])
    if with_final:
        h = _rms(h, fn_ref[...])
    o_ref[...] = h
    if with_cast:
        nin_o[...] = nin_ref[...].astype(BF16)
        nout_o[...] = nout_ref[...].astype(BF16)


def _layer_spec(stacked, layer):
    nd = stacked.ndim - 1
    return pl.BlockSpec((None,) + stacked.shape[1:], lambda *_: (layer,) + (0,) * nd,
                        pipeline_mode=pl.Buffered(1))


def _ffn(h, g, w_in, w_out, layer, ple=None, final_g=None, cast_next=None, *, name):
    m, d = h.shape
    d_ff = w_out.shape[0]
    tm = FFN_TM
    steps = m // tm
    row = lambda i: (i, 0)
    in_specs = [pl.BlockSpec((tm, d), row), _const_spec((1, d)),
                _const_spec(w_in.shape), _const_spec(w_out.shape)]
    args = [h, g.reshape(1, d), w_in, w_out]
    if ple is not None:
        p, pn, pgw, ppw = ple
        in_specs += [pl.BlockSpec((None, tm, p.shape[2]), lambda i: (layer, i, 0)), _const_spec((1, d)),
                     _layer_spec(pgw, layer), _layer_spec(ppw, layer)]
        args += [p, pn.reshape(1, d), pgw, ppw]
    if final_g is not None:
        in_specs.append(_const_spec((1, d)))
        args.append(final_g.reshape(1, d))
    out_specs = pl.BlockSpec((tm, d), row)
    out_shape = jax.ShapeDtypeStruct((m, d), F32)
    if cast_next is not None:
        n_in, n_out, nl = cast_next
        rin = n_in.shape[1] // steps
        n_ob = n_out.shape[1] // CAST_ROWS
        assert rin * steps == n_in.shape[1] and rin % BF16_ROWS == 0
        assert n_ob * CAST_ROWS == n_out.shape[1] and n_ob <= steps
        in_map = lambda i: (nl, i, 0)
        out_map = lambda i: (nl, jnp.minimum(i, n_ob - 1), 0)
        in_specs += [pl.BlockSpec((None, rin, n_in.shape[2]), in_map),
                     pl.BlockSpec((None, CAST_ROWS, n_out.shape[2]), out_map)]
        args += [n_in, n_out]
        out_specs = [out_specs, pl.BlockSpec((rin, n_in.shape[2]), lambda i: (i, 0)),
                     pl.BlockSpec((CAST_ROWS, n_out.shape[2]), lambda i: (jnp.minimum(i, n_ob - 1), 0))]
        out_shape = [out_shape, jax.ShapeDtypeStruct(n_in.shape[1:], BF16),
                     jax.ShapeDtypeStruct(n_out.shape[1:], BF16)]
    return pl.pallas_call(
        functools.partial(_ffn_kernel, d_ff=d_ff, fc=FFN_FC, with_ple=ple is not None,
                          with_final=final_g is not None, with_cast=cast_next is not None),
        grid=(steps,),
        in_specs=in_specs,
        out_specs=out_specs,
        out_shape=out_shape,
        compiler_params=pltpu.CompilerParams(dimension_semantics=("arbitrary",),
                                             vmem_limit_bytes=VMEM_LIMIT_BYTES),
        name=name,
    )(*args)


def _causal_conv_blocks(buf, out, w_ref, b_ref, *, taps, halo, ts, feeders, act=None):
    n_blocks = buf.shape[0]
    per_feeder = n_blocks // len(feeders)
    for j in range(n_blocks):
        if j % per_feeder == 0:
            feeders[j // per_feeder]()
        w = w_ref[j]
        for r0 in range(0, ts, CONV_ROWS):
            acc = jnp.broadcast_to(b_ref[j], (CONV_ROWS, LANES))
            for k in range(taps):
                off = halo - (taps - 1) + k + r0
                acc = acc + buf[j, off:off + CONV_ROWS, :] * w[k:k + 1, :]
            out[j, r0:r0 + CONV_ROWS, :] = acc if act is None else act(acc)
        buf[j, 0:halo, :] = buf[j, ts:ts + halo, :]


def _hyb_kernel(h_ref, g_ref, win_ref, wdt_ref, cw_ref, cb_ref, lng_ref, lnb_ref, scw_ref, scb_ref,
                dtb_ref, alog_ref, dskip_ref, snorm_ref, wout_ref, o_ref,
                ubuf, ycv, xbuf, xcv, state, *, ts):
    n_cb = ycv.shape[0]
    n_xb = xcv.shape[0]
    conv_ch = n_cb * LANES
    inner = SSM_HEADS * SSM_HEAD_DIM
    gw = inner // SSM_GROUPS
    n_ib = inner // LANES
    n_sb = SSM_STATE // LANES
    pw = PROJ_PIECE
    bpp = pw // LANES

    @pl.when(pl.program_id(1) == 0)
    def _():
        ubuf[:, 0:CONV_HALO, :] = jnp.zeros((n_cb, CONV_HALO, LANES), F32)
        xbuf[:, 0:SSM_HALO, :] = jnp.zeros((n_xb, SSM_HALO, LANES), F32)
        state[...] = jnp.zeros(state.shape, F32)

    h = h_ref[...]
    hn = _rms(h, g_ref[...]).astype(BF16)

    def glu(c):
        val = _dot(hn, win_ref[:, c * pw:(c + 1) * pw])
        gate = _dot(hn, win_ref[:, conv_ch + c * pw:conv_ch + (c + 1) * pw])
        u = val * jax.nn.sigmoid(gate)
        for jj in range(bpp):
            ubuf[c * bpp + jj, CONV_HALO:CONV_HALO + ts, :] = u[:, jj * LANES:(jj + 1) * LANES]

    def xbc_cols(c):
        base = 2 * conv_ch + inner
        x = _dot(hn, win_ref[:, base + c * pw:base + (c + 1) * pw])
        for jj in range(bpp):
            xbuf[c * bpp + jj, SSM_HALO:SSM_HALO + ts, :] = x[:, jj * LANES:(jj + 1) * LANES]

    _causal_conv_blocks(ubuf, ycv, cw_ref, cb_ref, taps=CONV_WIDTH, halo=CONV_HALO, ts=ts,
                        feeders=[functools.partial(glu, c) for c in range(n_cb // bpp)])
    yc = jnp.concatenate([ycv[j] for j in range(n_cb)], axis=1)
    mu = jnp.mean(yc, axis=-1, keepdims=True)
    ycc = yc - mu
    var = jnp.mean(ycc * ycc, axis=-1, keepdims=True)
    u_out = _silu(ycc * lax.rsqrt(var + EPS) * lng_ref[...] + lnb_ref[...]).astype(BF16)

    _causal_conv_blocks(xbuf, xcv, scw_ref, scb_ref, taps=SSM_CONV, halo=SSM_HALO, ts=ts, act=_silu,
                        feeders=[functools.partial(xbc_cols, c) for c in range(n_xb // bpp)])
    z = _dot(hn, win_ref[:, 2 * conv_ch:2 * conv_ch + inner])
    dt_raw = _dot(hn, wdt_ref[...]) + dtb_ref[...]
    dt = jnp.maximum(dt_raw, 0.0) + jnp.log1p(jnp.exp(-jnp.abs(dt_raw)))
    adt = dt * (-jnp.exp(alog_ref[...]))

    ri = lax.broadcasted_iota(jnp.int32, (SSM_CHUNK, SSM_CHUNK), 0)
    ci = lax.broadcasted_iota(jnp.int32, (SSM_CHUNK, SSM_CHUNK), 1)
    tri = ri >= ci
    tri_b = jnp.where(tri, 1.0, 0.0).astype(BF16)
    eh = lax.broadcasted_iota(jnp.int32, (LANES, inner), 0)
    el = lax.broadcasted_iota(jnp.int32, (LANES, inner), 1)
    in_head = jnp.logical_and(el >= eh * SSM_HEAD_DIM, el < (eh + 1) * SSM_HEAD_DIM)
    expand = jnp.where(in_head, 1.0, 0.0).astype(BF16)

    ys = []
    for c in range(ts // SSM_CHUNK):
        rows = slice(c * SSM_CHUNK, (c + 1) * SSM_CHUNK)
        dt_c = dt[rows]
        acs = sum(_dot(tri_b, p) for p in _split_bf16(adt[rows], 3))
        acs_t = acs.T
        last = acs[SSM_CHUNK - 1:SSM_CHUNK, :]
        stack = jnp.concatenate([dt_c, dt_c * jnp.exp(last - acs)], axis=0).astype(BF16)
        wide = _dot(stack, expand)
        dt_x = wide[0:SSM_CHUNK]
        w_end = wide[SSM_CHUNK:]
        sc_off = sum(_dot(p, expand) for p in _split_bf16(jnp.exp(acs), 2))
        xs_c = jnp.concatenate([xcv[j, rows, :] for j in range(n_ib)], axis=1)
        xdt = (xs_c * dt_x).astype(BF16)
        xend = (xs_c * w_end).astype(BF16)
        y_cols = []
        for g in range(SSM_GROUPS):
            b_g = jnp.concatenate([xcv[n_ib + g * n_sb + j, rows, :] for j in range(n_sb)], axis=1)
            c_g = jnp.concatenate([xcv[n_ib + (SSM_GROUPS + g) * n_sb + j, rows, :] for j in range(n_sb)],
                                  axis=1).astype(BF16)
            cb = _dot_nt(c_g, b_g.astype(BF16))
            st = state[g]
            y_off = _dot(c_g, st.astype(BF16)) * sc_off[:, g * gw:(g + 1) * gw]
            s_new = _dot(b_g.T.astype(BF16), xend[:, g * gw:(g + 1) * gw])
            state[g] = st * sc_off[SSM_CHUNK - 1:SSM_CHUNK, g * gw:(g + 1) * gw] + s_new
            lane = lax.broadcasted_iota(jnp.int32, (SSM_CHUNK, LANES), 1)
            low = lane < SSM_HEAD_DIM
            for pr in range(gw // LANES):
                col = g * gw + pr * LANES
                h0 = col // SSM_HEAD_DIM
                ws = []
                for hh in (h0, h0 + 1):
                    diff = acs[:, hh:hh + 1] - acs_t[hh:hh + 1, :]
                    ws.append((cb * jnp.exp(jnp.where(tri, diff, -jnp.inf))).astype(BF16))
                xcol = xdt[:, col:col + LANES]
                zero = jnp.zeros_like(xcol)
                rhs = jnp.concatenate([jnp.where(low, xcol, zero), jnp.where(low, zero, xcol)], axis=0)
                y_cols.append(_dot(jnp.concatenate(ws, axis=1), rhs)
                              + y_off[:, pr * LANES:(pr + 1) * LANES])
        ys.append(jnp.concatenate(y_cols, axis=1) + dskip_ref[...] * xs_c)
    y = jnp.concatenate(ys, axis=0) * _silu(z)
    yn = []
    for g in range(SSM_GROUPS):
        yg = y[:, g * gw:(g + 1) * gw]
        yn.append(yg * lax.rsqrt(jnp.mean(yg * yg, axis=-1, keepdims=True) + EPS))
    y_out = (jnp.concatenate(yn, axis=1) * snorm_ref[...]).astype(BF16)

    o_ref[...] = h + _dot(u_out, wout_ref[0:conv_ch, :]) + _dot(y_out, wout_ref[conv_ch:, :])


def _hybrid_mixer(h, bsz, norm_g, w_in, conv_w, conv_b, ln_g, ln_b, sc_w, sc_b, dt_bias, a_log, d_skip,
                  ssm_norm, w_out):
    m, d = h.shape
    ts = HYB_TS
    nt = m // bsz // ts
    conv_ch = conv_w.shape[1]
    xbc_w = sc_w.shape[1]
    inner = SSM_HEADS * SSM_HEAD_DIM
    n_main = w_in.shape[1] - SSM_HEADS
    pad_h = lambda v: jnp.pad(v.reshape(1, SSM_HEADS), ((0, 0), (0, LANES - SSM_HEADS)))
    w_dt = jnp.pad(w_in[:, n_main:], ((0, 0), (0, LANES - SSM_HEADS))).astype(BF16)
    taps_by_block = lambda w, rows: jnp.pad(w, ((0, rows - w.shape[0]), (0, 0))).reshape(
        rows, w.shape[1] // LANES, LANES).transpose(1, 0, 2)
    consts = [norm_g.reshape(1, d), w_in.astype(BF16), w_dt, taps_by_block(conv_w, CONV_HALO),
              conv_b.reshape(conv_ch // LANES, 1, LANES), ln_g.reshape(1, conv_ch), ln_b.reshape(1, conv_ch),
              taps_by_block(sc_w, SSM_HALO), sc_b.reshape(xbc_w // LANES, 1, LANES), pad_h(dt_bias), pad_h(a_log),
              jnp.repeat(d_skip, SSM_HEAD_DIM).reshape(1, inner), ssm_norm.reshape(1, inner),
              w_out.astype(BF16)]
    row = lambda b, s: (b * nt + s, 0)
    return pl.pallas_call(
        functools.partial(_hyb_kernel, ts=ts),
        grid=(bsz, nt),
        in_specs=[pl.BlockSpec((ts, d), row)] + [_const_spec(c.shape) for c in consts],
        out_specs=pl.BlockSpec((ts, d), row),
        out_shape=jax.ShapeDtypeStruct((m, d), F32),
        scratch_shapes=[
            pltpu.VMEM((conv_ch // LANES, CONV_HALO + ts, LANES), F32),
            pltpu.VMEM((conv_ch // LANES, ts, LANES), F32),
            pltpu.VMEM((xbc_w // LANES, SSM_HALO + ts, LANES), F32),
            pltpu.VMEM((xbc_w // LANES, ts, LANES), F32),
            pltpu.VMEM((SSM_GROUPS, SSM_STATE, inner // SSM_GROUPS), F32),
        ],
        compiler_params=pltpu.CompilerParams(dimension_semantics=("arbitrary", "arbitrary"),
                                             vmem_limit_bytes=VMEM_LIMIT_BYTES),
        name="hybrid_mixer",
    )(h, *consts)


def _attn_kernel(sinks_ref, h_ref, g_ref, wqkv_ref, bqkv_ref, cos_ref, sin_ref, rk_ref, rv_ref, wo_ref, bo_ref,
                 o_ref, kbuf, vbuf, att, *, ts):
    half = ATT_HEAD_DIM // 2
    grp = ATT_HEADS // ATT_KV_HEADS
    gq = grp * ATT_HEAD_DIM
    nq = ATT_HEADS * ATT_HEAD_DIM
    nkv = ATT_KV_HEADS * ATT_HEAD_DIM
    nblk = ts // WINDOW
    first_tile = pl.program_id(1) == 0

    @pl.when(first_tile)
    def _():
        kbuf[:, 0:WINDOW, :] = jnp.zeros((ATT_KV_HEADS, WINDOW, gq), BF16)
        vbuf[:, 0:WINDOW, :] = jnp.zeros((ATT_KV_HEADS, WINDOW, gq), BF16)

    h = h_ref[...]
    hn = _rms(h, g_ref[...]).astype(BF16)
    qkv = _dot(hn, wqkv_ref[...]) + bqkv_ref[...]
    cos = cos_ref[...]
    sin = sin_ref[...]

    def rope(x, y):
        return jnp.concatenate([x * cos - y * sin, y * cos + x * sin], axis=1)

    scale = ATT_HEAD_DIM ** -0.5
    q = [(rope(qkv[:, g * gq:g * gq + LANES], qkv[:, g * gq + LANES:(g + 1) * gq]) * scale).astype(BF16)
         for g in range(ATT_KV_HEADS)]
    k = rope(qkv[:, nq:nq + LANES], qkv[:, nq + LANES:nq + nkv]).astype(BF16)
    v = qkv[:, nq + nkv:].astype(BF16)
    krep = _dot(k, rk_ref[...]).astype(BF16)
    vrep = _dot(v, rv_ref[...]).astype(BF16)
    for g in range(ATT_KV_HEADS):
        kbuf[g, WINDOW:WINDOW + ts, :] = krep[:, g * gq:(g + 1) * gq]
        vbuf[g, WINDOW:WINDOW + ts, :] = vrep[:, g * gq:(g + 1) * gq]

    lane = lax.broadcasted_iota(jnp.int32, (WINDOW, gq), 1)
    lane_in = jnp.bitwise_and(lane, LANES - 1)
    q_slot = [jnp.where(jnp.logical_and(lane_in >= a * half, lane_in < (a + 1) * half), 1.0, 0.0).astype(BF16)
              for a in range(grp)]
    o_from = [lane >= a * ATT_HEAD_DIM for a in range(grp)]
    qi = jnp.bitwise_and(lax.broadcasted_iota(jnp.int32, (grp * WINDOW, WINDOW), 0), WINDOW - 1)
    kj = lax.broadcasted_iota(jnp.int32, (grp * WINDOW, WINDOW), 1)
    in_cur = kj <= qi
    has_prev = kj >= WINDOW * first_tile.astype(jnp.int32)
    srow = lax.broadcasted_iota(jnp.int32, (grp * WINDOW, 1), 0)

    for i in range(nblk):
        rows = slice(i * WINDOW, (i + 1) * WINDOW)
        keys = slice(i * WINDOW, (i + 2) * WINDOW)
        for g in range(ATT_KV_HEADS):
            qg = q[g][rows]
            q4 = jnp.concatenate([qg * q_slot[a] for a in range(grp)], axis=0)
            logits = _dot_nt(q4, kbuf[g, keys, :])
            prev = logits[:, 0:WINDOW]
            if i == 0:
                prev = jnp.where(has_prev, prev, -jnp.inf)
            lg = jnp.where(in_cur, logits[:, WINDOW:], prev)
            sink = sinks_ref[grp * g]
            for a in range(1, grp):
                sink = jnp.where(srow >= a * WINDOW, sinks_ref[grp * g + a], sink)
            mx = jnp.maximum(jnp.max(lg, axis=-1, keepdims=True), sink)
            e = jnp.exp(lg - mx)
            den = jnp.sum(e, axis=-1, keepdims=True) + jnp.exp(sink - mx)
            zero = jnp.zeros_like(e)
            p = jnp.concatenate([jnp.where(in_cur, zero, e), jnp.where(in_cur, e, zero)], axis=1).astype(BF16)
            o4 = _dot(p, vbuf[g, keys, :]) / den
            out = o4[0:WINDOW]
            for a in range(1, grp):
                out = jnp.where(o_from[a], o4[a * WINDOW:(a + 1) * WINDOW], out)
            att[rows, g * gq:(g + 1) * gq] = out.astype(BF16)

    for g in range(ATT_KV_HEADS):
        kbuf[g, 0:WINDOW, :] = krep[ts - WINDOW:ts, g * gq:(g + 1) * gq]
        vbuf[g, 0:WINDOW, :] = vrep[ts - WINDOW:ts, g * gq:(g + 1) * gq]
    o_ref[...] = h + _dot(att[...], wo_ref[...]) + bo_ref[...]


def _attention_mixer(h, bsz, norm_g, w_qkv, b_qkv, sinks, w_o, b_o):
    m, d = h.shape
    ts = ATT_TS
    seqlen = m // bsz
    nt = seqlen // ts
    half = ATT_HEAD_DIM // 2
    grp = ATT_HEADS // ATT_KV_HEADS
    gq = grp * ATT_HEAD_DIM
    nq, nkv = ATT_HEADS * ATT_HEAD_DIM, ATT_KV_HEADS * ATT_HEAD_DIM
    q_cols = [(grp * g + a) * ATT_HEAD_DIM + hf * half + dd
              for g in range(ATT_KV_HEADS) for hf in range(2) for a in range(grp) for dd in range(half)]
    k_cols = [nq + kh * ATT_HEAD_DIM + hf * half + dd
              for hf in range(2) for kh in range(ATT_KV_HEADS) for dd in range(half)]
    cols = jnp.asarray(q_cols + k_cols + list(range(nq + nkv, nq + 2 * nkv)), jnp.int32)
    src = jnp.arange(nkv)[:, None]
    dst = jnp.arange(ATT_KV_HEADS * gq)[None, :]
    g_of, in_g = dst // gq, dst % gq
    rk = (src == (in_g // LANES) * LANES + g_of * half + in_g % half).astype(BF16)
    rv = (src == g_of * ATT_HEAD_DIM + in_g % ATT_HEAD_DIM).astype(BF16)
    inv = ROPE_THETA ** (-jnp.arange(0, ATT_HEAD_DIM, 2, dtype=F32) / ATT_HEAD_DIM)
    ang = jnp.arange(seqlen, dtype=F32)[:, None] * inv[None, :]
    cos, sin = jnp.tile(jnp.cos(ang), (1, LANES // half)), jnp.tile(jnp.sin(ang), (1, LANES // half))
    row = lambda b, s: (b * nt + s, 0)
    tab = lambda b, s: (s, 0)
    consts = [norm_g.reshape(1, d), jnp.take(w_qkv, cols, axis=1).astype(BF16),
              jnp.take(b_qkv, cols).reshape(1, -1)]
    tail = [rk, rv, w_o.astype(BF16), b_o.reshape(1, d)]
    return pl.pallas_call(
        functools.partial(_attn_kernel, ts=ts),
        grid=(bsz, nt),
        in_specs=[pl.BlockSpec(memory_space=pltpu.SMEM), pl.BlockSpec((ts, d), row)]
                 + [_const_spec(c.shape) for c in consts]
                 + [pl.BlockSpec((ts, LANES), tab), pl.BlockSpec((ts, LANES), tab)]
                 + [_const_spec(c.shape) for c in tail],
        out_specs=pl.BlockSpec((ts, d), row),
        out_shape=jax.ShapeDtypeStruct((m, d), F32),
        scratch_shapes=[
            pltpu.VMEM((ATT_KV_HEADS, WINDOW + ts, gq), BF16),
            pltpu.VMEM((ATT_KV_HEADS, WINDOW + ts, gq), BF16),
            pltpu.VMEM((ts, nq), BF16),
        ],
        compiler_params=pltpu.CompilerParams(dimension_semantics=("arbitrary", "arbitrary"),
                                             vmem_limit_bytes=VMEM_LIMIT_BYTES),
        name="attention_mixer",
    )(sinks, h, *consts, cos, sin, *tail)


def kernel(x, p, norm_ffn1, ffn1_w_in, ffn1_w_out, norm_mix, norm_ffn2, ffn2_w_in, ffn2_w_out, ple_norm, ple_gate_w, ple_proj_w, hyb_w_in, conv_dw_w, conv_dw_b, conv_ln_g, conv_ln_b, ssm_conv_w, ssm_conv_b, ssm_dt_bias, ssm_a_log, ssm_d, ssm_norm, hyb_w_out, att_w_qkv, att_b_qkv, att_sinks, att_w_o, att_b_o, final_norm):
    bsz, seqlen, d = x.shape
    depth = p.shape[0]
    m = bsz * seqlen
    h = x.reshape(m, d)
    bf = lambda w: w.astype(BF16)
    ple_gw, ple_pw = bf(ple_gate_w), bf(ple_proj_w)
    p_rows = p.reshape(depth, m, -1)
    w_in, w_out = bf(ffn1_w_in[0]), bf(ffn1_w_out[0])

    for i in range(depth):
        j = i // 2
        h, w_in, w_out = _ffn(h, norm_ffn1[i], w_in, w_out, i, cast_next=(ffn2_w_in, ffn2_w_out, i),
                              name=f"ffn1_l{i}")
        if i % 2 == 0:
            h = _hybrid_mixer(h, bsz, norm_mix[i], hyb_w_in[j], conv_dw_w[j], conv_dw_b[j], conv_ln_g[j],
                              conv_ln_b[j], ssm_conv_w[j], ssm_conv_b[j], ssm_dt_bias[j], ssm_a_log[j],
                              ssm_d[j], ssm_norm[j], hyb_w_out[j])
        else:
            h = _attention_mixer(h, bsz, norm_mix[i], att_w_qkv[j], att_b_qkv[j], att_sinks[j],
                                 att_w_o[j], att_b_o[j])
        ple = (p_rows, ple_norm[i], ple_gw, ple_pw)
        if i + 1 < depth:
            h, w_in, w_out = _ffn(h, norm_ffn2[i], w_in, w_out, i, ple=ple,
                                  cast_next=(ffn1_w_in, ffn1_w_out, i + 1), name=f"ffn2_l{i}")
        else:
            h = _ffn(h, norm_ffn2[i], w_in, w_out, i, ple=ple, final_g=final_norm, name=f"ffn2_l{i}")
    return h.reshape(bsz, seqlen, d)
```

```python
import functools

import jax
import jax.numpy as jnp
from jax import lax
from jax.experimental import pallas as pl
from jax.experimental.pallas import tpu as pltpu

F32 = jnp.float32
BF16 = jnp.bfloat16

EPS = 1e-6
LANES = 128
SUBLANES = 8
VMEM_LIMIT_BYTES = 56 * 1024 * 1024

CONV_WIDTH = 31
CONV_HALO = 32
SSM_HEADS = 16
SSM_HEAD_DIM = 64
SSM_GROUPS = 2
SSM_STATE = 128
SSM_CONV = 4
SSM_HALO = 8
SSM_CHUNK = 128
ATT_HEADS = 16
ATT_KV_HEADS = 4
ATT_HEAD_DIM = 64
WINDOW = 128
ROPE_THETA = 10000.0

FFN_TM = 1024
CAST_ROWS = 128
BF16_ROWS = 16
FFN_FC = 256
HYB_TS = 512
CONV_ROWS = 256
PROJ_PIECE = 256
ATT_TS = 512


def _rms(x, g):
    return x * lax.rsqrt(jnp.mean(x * x, axis=-1, keepdims=True) + EPS) * g


def _silu(x):
    return x * jax.nn.sigmoid(x)


def _dot(a, b):
    return jnp.dot(a, b, preferred_element_type=F32)


def _dot_nt(a, b):
    return lax.dot_general(a, b, (((1,), (1,)), ((), ())), preferred_element_type=F32)


def _const_spec(shape):
    nd = len(shape)
    return pl.BlockSpec(shape, lambda *_: (0,) * nd, pipeline_mode=pl.Buffered(1))


def _split_bf16(x, parts):
    out = []
    r = x
    for _ in range(parts):
        p = r.astype(BF16)
        out.append(p)
        r = r - p.astype(F32)
    return out


def _ffn_kernel(*refs, d_ff, fc, with_ple, with_final, n_cast):
    h_ref, g_ref, win_ref, wout_ref = refs[:4]
    rest = list(refs[4:])
    if with_ple:
        p_ref, pn_ref, pgw_ref, ppw_ref = rest[:4]
        rest = rest[4:]
    if with_final:
        fn_ref = rest[0]
        rest = rest[1:]
    cast_in, o_ref, cast_out = rest[:n_cast], rest[n_cast], rest[n_cast + 1:]

    h = h_ref[...]
    xn = _rms(h, g_ref[...]).astype(BF16)
    acc = jnp.zeros(h.shape, F32)
    for c in range(d_ff // fc):
        gate = _dot(xn, win_ref[:, c * fc:(c + 1) * fc])
        up = _dot(xn, win_ref[:, d_ff + c * fc:d_ff + (c + 1) * fc])
        act = (_silu(gate) * up).astype(BF16)
        acc = acc + _dot(act, wout_ref[c * fc:(c + 1) * fc, :])
    h = h + 0.5 * acc
    if with_ple:
        gate = jax.nn.sigmoid(_dot(_rms(h, pn_ref[...]).astype(BF16), pgw_ref[...]))
        h = h + gate * _dot(p_ref[...].astype(BF16), ppw_ref[...])
    if with_final:
        h = _rms(h, fn_ref[...])
    o_ref[...] = h
    for src, dst in zip(cast_in, cast_out):
        dst[...] = src[:, 0:dst.shape[1]].astype(BF16)


def _layer_spec(stacked, layer):
    nd = stacked.ndim - 1
    return pl.BlockSpec((None,) + stacked.shape[1:], lambda *_: (layer,) + (0,) * nd,
                        pipeline_mode=pl.Buffered(1))


def _ffn(h, g, w_in, w_out, layer, ple=None, final_g=None, cast_next=None, cast_also=(), *, name):
    m, d = h.shape
    d_ff = w_out.shape[0]
    tm = FFN_TM
    steps = m // tm
    row = lambda i: (i, 0)
    in_specs = [pl.BlockSpec((tm, d), row), _const_spec((1, d)),
                _const_spec(w_in.shape), _const_spec(w_out.shape)]
    args = [h, g.reshape(1, d), w_in, w_out]
    if ple is not None:
        p, pn, pgw, ppw = ple
        in_specs += [pl.BlockSpec((None, tm, p.shape[2]), lambda i: (layer, i, 0)), _const_spec((1, d)),
                     _layer_spec(pgw, layer), _layer_spec(ppw, layer)]
        args += [p, pn.reshape(1, d), pgw, ppw]
    if final_g is not None:
        in_specs.append(_const_spec((1, d)))
        args.append(final_g.reshape(1, d))
    out_specs = [pl.BlockSpec((tm, d), row)]
    out_shape = [jax.ShapeDtypeStruct((m, d), F32)]
    n_cast = 0
    if cast_next is not None:
        n_in, n_out, nl = cast_next
        rin = n_in.shape[1] // steps
        n_ob = n_out.shape[1] // CAST_ROWS
        assert rin * steps == n_in.shape[1] and rin % BF16_ROWS == 0
        assert n_ob * CAST_ROWS == n_out.shape[1] and n_ob <= steps
        in_map = lambda i: (nl, i, 0)
        out_map = lambda i: (nl, jnp.minimum(i, n_ob - 1), 0)
        in_specs += [pl.BlockSpec((None, rin, n_in.shape[2]), in_map),
                     pl.BlockSpec((None, CAST_ROWS, n_out.shape[2]), out_map)]
        args += [n_in, n_out]
        out_specs += [pl.BlockSpec((rin, n_in.shape[2]), lambda i: (i, 0)),
                      pl.BlockSpec((CAST_ROWS, n_out.shape[2]), lambda i: (jnp.minimum(i, n_ob - 1), 0))]
        out_shape += [jax.ShapeDtypeStruct(n_in.shape[1:], BF16), jax.ShapeDtypeStruct(n_out.shape[1:], BF16)]
        n_cast += 2
    for mat, keep in cast_also:
        rows = mat.shape[0] // steps
        assert rows * steps == mat.shape[0] and rows % BF16_ROWS == 0 and keep % LANES == 0
        in_specs.append(pl.BlockSpec((rows, mat.shape[1]), row))
        args.append(mat)
        out_specs.append(pl.BlockSpec((rows, keep), row))
        out_shape.append(jax.ShapeDtypeStruct((mat.shape[0], keep), BF16))
        n_cast += 1
    return pl.pallas_call(
        functools.partial(_ffn_kernel, d_ff=d_ff, fc=FFN_FC, with_ple=ple is not None,
                          with_final=final_g is not None, n_cast=n_cast),
        grid=(steps,),
        in_specs=in_specs,
        out_specs=out_specs,
        out_shape=out_shape,
        compiler_params=pltpu.CompilerParams(dimension_semantics=("arbitrary",),
                                             vmem_limit_bytes=VMEM_LIMIT_BYTES),
        name=name,
    )(*args)


def _causal_conv_blocks(buf, out, w_ref, b_ref, *, taps, halo, ts, feeders, act=None):
    n_blocks = buf.shape[0]
    per_feeder = n_blocks // len(feeders)
    for j in range(n_blocks):
        if j % per_feeder == 0:
            feeders[j // per_feeder]()
        w = w_ref[j]
        for r0 in range(0, ts, CONV_ROWS):
            acc = jnp.broadcast_to(b_ref[j], (CONV_ROWS, LANES))
            for k in range(taps):
                off = halo - (taps - 1) + k + r0
                acc = acc + buf[j, off:off + CONV_ROWS, :] * w[k:k + 1, :]
            out[j, r0:r0 + CONV_ROWS, :] = acc if act is None else act(acc)
        buf[j, 0:halo, :] = buf[j, ts:ts + halo, :]


def _hyb_kernel(h_ref, g_ref, win_ref, wdt_ref, cw_ref, cb_ref, lng_ref, lnb_ref, scw_ref, scb_ref,
                dtb_ref, alog_ref, dskip_ref, snorm_ref, wout_ref, o_ref,
                ubuf, ycv, xbuf, xcv, state, *, ts):
    n_cb = ycv.shape[0]
    n_xb = xcv.shape[0]
    conv_ch = n_cb * LANES
    inner = SSM_HEADS * SSM_HEAD_DIM
    gw = inner // SSM_GROUPS
    n_ib = inner // LANES
    n_sb = SSM_STATE // LANES
    pw = PROJ_PIECE
    bpp = pw // LANES

    @pl.when(pl.program_id(1) == 0)
    def _():
        ubuf[:, 0:CONV_HALO, :] = jnp.zeros((n_cb, CONV_HALO, LANES), F32)
        xbuf[:, 0:SSM_HALO, :] = jnp.zeros((n_xb, SSM_HALO, LANES), F32)
        state[...] = jnp.zeros(state.shape, F32)

    h = h_ref[...]
    hn = _rms(h, g_ref[...]).astype(BF16)

    def glu(c):
        val = _dot(hn, win_ref[:, c * pw:(c + 1) * pw])
        gate = _dot(hn, win_ref[:, conv_ch + c * pw:conv_ch + (c + 1) * pw])
        u = val * jax.nn.sigmoid(gate)
        for jj in range(bpp):
            ubuf[c * bpp + jj, CONV_HALO:CONV_HALO + ts, :] = u[:, jj * LANES:(jj + 1) * LANES]

    def xbc_cols(c):
        base = 2 * conv_ch + inner
        x = _dot(hn, win_ref[:, base + c * pw:base + (c + 1) * pw])
        for jj in range(bpp):
            xbuf[c * bpp + jj, SSM_HALO:SSM_HALO + ts, :] = x[:, jj * LANES:(jj + 1) * LANES]

    _causal_conv_blocks(ubuf, ycv, cw_ref, cb_ref, taps=CONV_WIDTH, halo=CONV_HALO, ts=ts,
                        feeders=[functools.partial(glu, c) for c in range(n_cb // bpp)])
    yc = jnp.concatenate([ycv[j] for j in range(n_cb)], axis=1)
    mu = jnp.mean(yc, axis=-1, keepdims=True)
    ycc = yc - mu
    var = jnp.mean(ycc * ycc, axis=-1, keepdims=True)
    u_out = _silu(ycc * lax.rsqrt(var + EPS) * lng_ref[...] + lnb_ref[...]).astype(BF16)

    _causal_conv_blocks(xbuf, xcv, scw_ref, scb_ref, taps=SSM_CONV, halo=SSM_HALO, ts=ts, act=_silu,
                        feeders=[functools.partial(xbc_cols, c) for c in range(n_xb // bpp)])
    z = _dot(hn, win_ref[:, 2 * conv_ch:2 * conv_ch + inner])
    dt_raw = _dot(hn, wdt_ref[...]) + dtb_ref[...]
    dt = jnp.maximum(dt_raw, 0.0) + jnp.log1p(jnp.exp(-jnp.abs(dt_raw)))
    adt = dt * (-jnp.exp(alog_ref[...]))

    ri = lax.broadcasted_iota(jnp.int32, (SSM_CHUNK, SSM_CHUNK), 0)
    ci = lax.broadcasted_iota(jnp.int32, (SSM_CHUNK, SSM_CHUNK), 1)
    tri = ri >= ci
    tri_b = jnp.where(tri, 1.0, 0.0).astype(BF16)
    eh = lax.broadcasted_iota(jnp.int32, (LANES, inner), 0)
    el = lax.broadcasted_iota(jnp.int32, (LANES, inner), 1)
    in_head = jnp.logical_and(el >= eh * SSM_HEAD_DIM, el < (eh + 1) * SSM_HEAD_DIM)
    expand = jnp.where(in_head, 1.0, 0.0).astype(BF16)

    ys = []
    for c in range(ts // SSM_CHUNK):
        rows = slice(c * SSM_CHUNK, (c + 1) * SSM_CHUNK)
        dt_c = dt[rows]
        acs = sum(_dot(tri_b, p) for p in _split_bf16(adt[rows], 3))
        acs_t = acs.T
        last = acs[SSM_CHUNK - 1:SSM_CHUNK, :]
        stack = jnp.concatenate([dt_c, dt_c * jnp.exp(last - acs)], axis=0).astype(BF16)
        wide = _dot(stack, expand)
        dt_x = wide[0:SSM_CHUNK]
        w_end = wide[SSM_CHUNK:]
        sc_off = sum(_dot(p, expand) for p in _split_bf16(jnp.exp(acs), 2))
        xs_c = jnp.concatenate([xcv[j, rows, :] for j in range(n_ib)], axis=1)
        xdt = (xs_c * dt_x).astype(BF16)
        xend = (xs_c * w_end).astype(BF16)
        y_cols = []
        for g in range(SSM_GROUPS):
            b_g = jnp.concatenate([xcv[n_ib + g * n_sb + j, rows, :] for j in range(n_sb)], axis=1)
            c_g = jnp.concatenate([xcv[n_ib + (SSM_GROUPS + g) * n_sb + j, rows, :] for j in range(n_sb)],
                                  axis=1).astype(BF16)
            cb = _dot_nt(c_g, b_g.astype(BF16))
            st = state[g]
            y_off = _dot(c_g, st.astype(BF16)) * sc_off[:, g * gw:(g + 1) * gw]
            s_new = _dot(b_g.T.astype(BF16), xend[:, g * gw:(g + 1) * gw])
            state[g] = st * sc_off[SSM_CHUNK - 1:SSM_CHUNK, g * gw:(g + 1) * gw] + s_new
            lane = lax.broadcasted_iota(jnp.int32, (SSM_CHUNK, LANES), 1)
            low = lane < SSM_HEAD_DIM
            for pr in range(gw // LANES):
                col = g * gw + pr * LANES
                h0 = col // SSM_HEAD_DIM
                ws = []
                for hh in (h0, h0 + 1):
                    diff = acs[:, hh:hh + 1] - acs_t[hh:hh + 1, :]
                    ws.append((cb * jnp.exp(jnp.where(tri, diff, -jnp.inf))).astype(BF16))
                xcol = xdt[:, col:col + LANES]
                zero = jnp.zeros_like(xcol)
                rhs = jnp.concatenate([jnp.where(low, xcol, zero), jnp.where(low, zero, xcol)], axis=0)
                y_cols.append(_dot(jnp.concatenate(ws, axis=1), rhs)
                              + y_off[:, pr * LANES:(pr + 1) * LANES])
        ys.append(jnp.concatenate(y_cols, axis=1) + dskip_ref[...] * xs_c)
    y = jnp.concatenate(ys, axis=0) * _silu(z)
    yn = []
    for g in range(SSM_GROUPS):
        yg = y[:, g * gw:(g + 1) * gw]
        yn.append(yg * lax.rsqrt(jnp.mean(yg * yg, axis=-1, keepdims=True) + EPS))
    y_out = (jnp.concatenate(yn, axis=1) * snorm_ref[...]).astype(BF16)

    o_ref[...] = h + _dot(u_out, wout_ref[0:conv_ch, :]) + _dot(y_out, wout_ref[conv_ch:, :])


def _hybrid_mixer(h, bsz, norm_g, w_in, conv_w, conv_b, ln_g, ln_b, sc_w, sc_b, dt_bias, a_log, d_skip,
                  ssm_norm, w_main_bf, w_out_bf):
    m, d = h.shape
    ts = HYB_TS
    nt = m // bsz // ts
    conv_ch = conv_w.shape[1]
    xbc_w = sc_w.shape[1]
    inner = SSM_HEADS * SSM_HEAD_DIM
    n_main = w_in.shape[1] - SSM_HEADS
    pad_h = lambda v: jnp.pad(v.reshape(1, SSM_HEADS), ((0, 0), (0, LANES - SSM_HEADS)))
    w_dt = jnp.pad(w_in[:, n_main:], ((0, 0), (0, LANES - SSM_HEADS))).astype(BF16)
    taps_by_block = lambda w, rows: jnp.pad(w, ((0, rows - w.shape[0]), (0, 0))).reshape(
        rows, w.shape[1] // LANES, LANES).transpose(1, 0, 2)
    consts = [norm_g.reshape(1, d), w_main_bf, w_dt, taps_by_block(conv_w, CONV_HALO),
              conv_b.reshape(conv_ch // LANES, 1, LANES), ln_g.reshape(1, conv_ch), ln_b.reshape(1, conv_ch),
              taps_by_block(sc_w, SSM_HALO), sc_b.reshape(xbc_w // LANES, 1, LANES), pad_h(dt_bias), pad_h(a_log),
              jnp.repeat(d_skip, SSM_HEAD_DIM).reshape(1, inner), ssm_norm.reshape(1, inner),
              w_out_bf]
    row = lambda b, s: (b * nt + s, 0)
    return pl.pallas_call(
        functools.partial(_hyb_kernel, ts=ts),
        grid=(bsz, nt),
        in_specs=[pl.BlockSpec((ts, d), row)] + [_const_spec(c.shape) for c in consts],
        out_specs=pl.BlockSpec((ts, d), row),
        out_shape=jax.ShapeDtypeStruct((m, d), F32),
        scratch_shapes=[
            pltpu.VMEM((conv_ch // LANES, CONV_HALO + ts, LANES), F32),
            pltpu.VMEM((conv_ch // LANES, ts, LANES), F32),
            pltpu.VMEM((xbc_w // LANES, SSM_HALO + ts, LANES), F32),
            pltpu.VMEM((xbc_w // LANES, ts, LANES), F32),
            pltpu.VMEM((SSM_GROUPS, SSM_STATE, inner // SSM_GROUPS), F32),
        ],
        compiler_params=pltpu.CompilerParams(dimension_semantics=("arbitrary", "arbitrary"),
                                             vmem_limit_bytes=VMEM_LIMIT_BYTES),
        name="hybrid_mixer",
    )(h, *consts)


def _attn_kernel(sinks_ref, h_ref, g_ref, wqkv_ref, bqkv_ref, cos_ref, sin_ref, rk_ref, rv_ref, wo_ref, bo_ref,
                 o_ref, kbuf, vbuf, att, *, ts):
    half = ATT_HEAD_DIM // 2
    grp = ATT_HEADS // ATT_KV_HEADS
    gq = grp * ATT_HEAD_DIM
    nq = ATT_HEADS * ATT_HEAD_DIM
    nkv = ATT_KV_HEADS * ATT_HEAD_DIM
    nblk = ts // WINDOW
    first_tile = pl.program_id(1) == 0

    @pl.when(first_tile)
    def _():
        kbuf[:, 0:WINDOW, :] = jnp.zeros((ATT_KV_HEADS, WINDOW, gq), BF16)
        vbuf[:, 0:WINDOW, :] = jnp.zeros((ATT_KV_HEADS, WINDOW, gq), BF16)

    h = h_ref[...]
    hn = _rms(h, g_ref[...]).astype(BF16)
    qkv = _dot(hn, wqkv_ref[...]) + bqkv_ref[...]
    cos = cos_ref[...]
    sin = sin_ref[...]

    def rope(x, y):
        return jnp.concatenate([x * cos - y * sin, y * cos + x * sin], axis=1)

    scale = ATT_HEAD_DIM ** -0.5
    q = [(rope(qkv[:, g * gq:g * gq + LANES], qkv[:, g * gq + LANES:(g + 1) * gq]) * scale).astype(BF16)
         for g in range(ATT_KV_HEADS)]
    k = rope(qkv[:, nq:nq + LANES], qkv[:, nq + LANES:nq + nkv]).astype(BF16)
    v = qkv[:, nq + nkv:].astype(BF16)
    krep = _dot(k, rk_ref[...]).astype(BF16)
    vrep = _dot(v, rv_ref[...]).astype(BF16)
    for g in range(ATT_KV_HEADS):
        kbuf[g, WINDOW:WINDOW + ts, :] = krep[:, g * gq:(g + 1) * gq]
        vbuf[g, WINDOW:WINDOW + ts, :] = vrep[:, g * gq:(g + 1) * gq]

    lane = lax.broadcasted_iota(jnp.int32, (WINDOW, gq), 1)
    lane_in = jnp.bitwise_and(lane, LANES - 1)
    q_slot = [jnp.where(jnp.logical_and(lane_in >= a * half, lane_in < (a + 1) * half), 1.0, 0.0).astype(BF16)
              for a in range(grp)]
    o_from = [lane >= a * ATT_HEAD_DIM for a in range(grp)]
    qi = jnp.bitwise_and(lax.broadcasted_iota(jnp.int32, (grp * WINDOW, WINDOW), 0), WINDOW - 1)
    kj = lax.broadcasted_iota(jnp.int32, (grp * WINDOW, WINDOW), 1)
    in_cur = kj <= qi
    has_prev = kj >= WINDOW * first_tile.astype(jnp.int32)
    srow = lax.broadcasted_iota(jnp.int32, (grp * WINDOW, 1), 0)

    for i in range(nblk):
        rows = slice(i * WINDOW, (i + 1) * WINDOW)
        keys = slice(i * WINDOW, (i + 2) * WINDOW)
        for g in range(ATT_KV_HEADS):
            qg = q[g][rows]
            q4 = jnp.concatenate([qg * q_slot[a] for a in range(grp)], axis=0)
            logits = _dot_nt(q4, kbuf[g, keys, :])
            prev = logits[:, 0:WINDOW]
            if i == 0:
                prev = jnp.where(has_prev, prev, -jnp.inf)
            lg = jnp.where(in_cur, logits[:, WINDOW:], prev)
            sink = sinks_ref[grp * g]
            for a in range(1, grp):
                sink = jnp.where(srow >= a * WINDOW, sinks_ref[grp * g + a], sink)
            mx = jnp.maximum(jnp.max(lg, axis=-1, keepdims=True), sink)
            e = jnp.exp(lg - mx)
            den = jnp.sum(e, axis=-1, keepdims=True) + jnp.exp(sink - mx)
            zero = jnp.zeros_like(e)
            p = jnp.concatenate([jnp.where(in_cur, zero, e), jnp.where(in_cur, e, zero)], axis=1).astype(BF16)
            o4 = _dot(p, vbuf[g, keys, :]) / den
            out = o4[0:WINDOW]
            for a in range(1, grp):
                out = jnp.where(o_from[a], o4[a * WINDOW:(a + 1) * WINDOW], out)
            att[rows, g * gq:(g + 1) * gq] = out.astype(BF16)

    for g in range(ATT_KV_HEADS):
        kbuf[g, 0:WINDOW, :] = krep[ts - WINDOW:ts, g * gq:(g + 1) * gq]
        vbuf[g, 0:WINDOW, :] = vrep[ts - WINDOW:ts, g * gq:(g + 1) * gq]
    o_ref[...] = h + _dot(att[...], wo_ref[...]) + bo_ref[...]


def _attention_mixer(h, bsz, norm_g, w_qkv, b_qkv, sinks, w_o_bf, b_o):
    m, d = h.shape
    ts = ATT_TS
    seqlen = m // bsz
    nt = seqlen // ts
    half = ATT_HEAD_DIM // 2
    grp = ATT_HEADS // ATT_KV_HEADS
    gq = grp * ATT_HEAD_DIM
    nq, nkv = ATT_HEADS * ATT_HEAD_DIM, ATT_KV_HEADS * ATT_HEAD_DIM
    q_cols = [(grp * g + a) * ATT_HEAD_DIM + hf * half + dd
              for g in range(ATT_KV_HEADS) for hf in range(2) for a in range(grp) for dd in range(half)]
    k_cols = [nq + kh * ATT_HEAD_DIM + hf * half + dd
              for hf in range(2) for kh in range(ATT_KV_HEADS) for dd in range(half)]
    cols = jnp.asarray(q_cols + k_cols + list(range(nq + nkv, nq + 2 * nkv)), jnp.int32)
    src = jnp.arange(nkv)[:, None]
    dst = jnp.arange(ATT_KV_HEADS * gq)[None, :]
    g_of, in_g = dst // gq, dst % gq
    rk = (src == (in_g // LANES) * LANES + g_of * half + in_g % half).astype(BF16)
    rv = (src == g_of * ATT_HEAD_DIM + in_g % ATT_HEAD_DIM).astype(BF16)
    inv = ROPE_THETA ** (-jnp.arange(0, ATT_HEAD_DIM, 2, dtype=F32) / ATT_HEAD_DIM)
    ang = jnp.arange(seqlen, dtype=F32)[:, None] * inv[None, :]
    cos, sin = jnp.tile(jnp.cos(ang), (1, LANES // half)), jnp.tile(jnp.sin(ang), (1, LANES // half))
    row = lambda b, s: (b * nt + s, 0)
    tab = lambda b, s: (s, 0)
    consts = [norm_g.reshape(1, d), jnp.take(w_qkv, cols, axis=1).astype(BF16),
              jnp.take(b_qkv, cols).reshape(1, -1)]
    tail = [rk, rv, w_o_bf, b_o.reshape(1, d)]
    return pl.pallas_call(
        functools.partial(_attn_kernel, ts=ts),
        grid=(bsz, nt),
        in_specs=[pl.BlockSpec(memory_space=pltpu.SMEM), pl.BlockSpec((ts, d), row)]
                 + [_const_spec(c.shape) for c in consts]
                 + [pl.BlockSpec((ts, LANES), tab), pl.BlockSpec((ts, LANES), tab)]
                 + [_const_spec(c.shape) for c in tail],
        out_specs=pl.BlockSpec((ts, d), row),
        out_shape=jax.ShapeDtypeStruct((m, d), F32),
        scratch_shapes=[
            pltpu.VMEM((ATT_KV_HEADS, WINDOW + ts, gq), BF16),
            pltpu.VMEM((ATT_KV_HEADS, WINDOW + ts, gq), BF16),
            pltpu.VMEM((ts, nq), BF16),
        ],
        compiler_params=pltpu.CompilerParams(dimension_semantics=("arbitrary", "arbitrary"),
                                             vmem_limit_bytes=VMEM_LIMIT_BYTES),
        name="attention_mixer",
    )(sinks, h, *consts, cos, sin, *tail)


def kernel(x, p, norm_ffn1, ffn1_w_in, ffn1_w_out, norm_mix, norm_ffn2, ffn2_w_in, ffn2_w_out, ple_norm, ple_gate_w, ple_proj_w, hyb_w_in, conv_dw_w, conv_dw_b, conv_ln_g, conv_ln_b, ssm_conv_w, ssm_conv_b, ssm_dt_bias, ssm_a_log, ssm_d, ssm_norm, hyb_w_out, att_w_qkv, att_b_qkv, att_sinks, att_w_o, att_b_o, final_norm):
    bsz, seqlen, d = x.shape
    depth = p.shape[0]
    m = bsz * seqlen
    h = x.reshape(m, d)
    bf = lambda w: w.astype(BF16)
    ple_gw, ple_pw = bf(ple_gate_w), bf(ple_proj_w)
    p_rows = p.reshape(depth, m, -1)
    w_in, w_out = bf(ffn1_w_in[0]), bf(ffn1_w_out[0])

    for i in range(depth):
        j = i // 2
        if i % 2 == 0:
            mixer_w = [(hyb_w_in[j], hyb_w_in.shape[2] - SSM_HEADS), (hyb_w_out[j], hyb_w_out.shape[2])]
        else:
            mixer_w = [(att_w_o[j], att_w_o.shape[2])]
        h, w_in, w_out, *mixer_bf = _ffn(h, norm_ffn1[i], w_in, w_out, i, cast_next=(ffn2_w_in, ffn2_w_out, i),
                                         cast_also=mixer_w, name=f"ffn1_l{i}")
        if i % 2 == 0:
            h = _hybrid_mixer(h, bsz, norm_mix[i], hyb_w_in[j], conv_dw_w[j], conv_dw_b[j], conv_ln_g[j],
                              conv_ln_b[j], ssm_conv_w[j], ssm_conv_b[j], ssm_dt_bias[j], ssm_a_log[j],
                              ssm_d[j], ssm_norm[j], *mixer_bf)
        else:
            h = _attention_mixer(h, bsz, norm_mix[i], att_w_qkv[j], att_b_qkv[j], att_sinks[j],
                                 mixer_bf[0], att_b_o[j])
        ple = (p_rows, ple_norm[i], ple_gw, ple_pw)
        if i + 1 < depth:
            h, w_in, w_out = _ffn(h, norm_ffn2[i], w_in, w_out, i, ple=ple,
                                  cast_next=(ffn1_w_in, ffn1_w_out, i + 1), name=f"ffn2_l{i}")
        else:
            (h,) = _ffn(h, norm_ffn2[i], w_in, w_out, i, ple=ple, final_g=final_norm, name=f"ffn2_l{i}")
    return h.reshape(bsz, seqlen, d)
```
